```python
import numpy as np
import jax
import jax.numpy as jnp
from jax import lax

D_MODEL = 2048
BATCH = 1
SEQ = 8192
DEPTH = 1
DEC_BATCH = 128
DEC_SEQ = 1
PAST_LEN = 16384
PAGE_SIZE = 128

MLA_HEADS = 8
QK_NOPE = 128
QK_ROPE = 64
V_HEAD = 128
Q_RANK = 512
KV_RANK = 512
ROPE_BASE = 10000.0
MLA_SCALE = (QK_NOPE + QK_ROPE) ** -0.5
Q_BLOCK = 128
GLA_HEADS = 4
GLA_DK = 128
GLA_DV = 256
GATE_RANK = 16
GATE_NORM = 16.0
GLA_CHUNK = 64
D_FF = 5632
CONV_W = 3
EPS = 1e-6

MIX_MLA = MLA_HEADS * V_HEAD
MIX_GLA = GLA_HEADS * GLA_DV
D_MIX = MIX_MLA + MIX_GLA
GLA_QK = GLA_HEADS * GLA_DK
IN_SPLIT_SIZES = (Q_RANK, KV_RANK, QK_ROPE, GLA_QK, GLA_QK, MIX_GLA, GATE_RANK, MIX_GLA)
D_IN = sum(IN_SPLIT_SIZES)

kernel_name = 'hymba_mla_gla_convffn_step'


def _rmsnorm(x, w):
    xf = x.astype(jnp.float32)
    y = xf * lax.rsqrt(jnp.mean(xf * xf, axis=-1, keepdims=True) + EPS)
    return (y * w.astype(jnp.float32)).astype(x.dtype)


def _rope(x, pos):
    half = QK_ROPE // 2
    inv = ROPE_BASE ** (-2.0 * jnp.arange(half, dtype=jnp.float32) / QK_ROPE)
    ang = pos[:, None] * inv[None, :]
    cos = jnp.cos(ang)[None, :, None, :]
    sin = jnp.sin(ang)[None, :, None, :]
    xf = x.astype(jnp.float32)
    x1, x2 = xf[..., :half], xf[..., half:]
    return jnp.concatenate([x1 * cos - x2 * sin, x1 * sin + x2 * cos], axis=-1).astype(x.dtype)


def _project(h, pos, w_in, q_norm, w_uq, kv_norm):
    B, T, _ = h.shape
    idx = np.cumsum(IN_SPLIT_SIZES)[:-1].tolist()
    cq, ckv, kr, gq, gk, gv, ga, gg = jnp.split(h @ w_in, idx, axis=-1)
    q = (_rmsnorm(cq, q_norm) @ w_uq).reshape(B, T, MLA_HEADS, QK_NOPE + QK_ROPE)
    q_nope = q[..., :QK_NOPE]
    q_rope = _rope(q[..., QK_NOPE:], pos)
    c_kv = _rmsnorm(ckv, kv_norm)
    k_rope = _rope(kr[:, :, None, :], pos)[:, :, 0, :]
    return q_nope, q_rope, c_kv, k_rope, (gq, gk, gv, ga, gg)


def _mla_prompt(q_nope, q_rope, c_kv, k_rope, w_uk, w_uv):
    B, S, H, _ = q_nope.shape
    nb = S // Q_BLOCK
    k_nope = jnp.einsum('bsc,chd->bshd', c_kv, w_uk)
    v = jnp.einsum('bsc,chd->bshd', c_kv, w_uv)
    qn = q_nope.reshape(B, nb, Q_BLOCK, H, QK_NOPE).transpose(1, 0, 2, 3, 4)
    qr = q_rope.reshape(B, nb, Q_BLOCK, H, QK_ROPE).transpose(1, 0, 2, 3, 4)
    k_pos = jnp.arange(S)

    def block(args):
        qn_b, qr_b, i = args
        s = jnp.einsum('bqhd,bkhd->bhqk', qn_b, k_nope) + jnp.einsum('bqhr,bkr->bhqk', qr_b, k_rope)
        q_pos = i * Q_BLOCK + jnp.arange(Q_BLOCK)
        s = jnp.where(k_pos[None, :] <= q_pos[:, None], s.astype(jnp.float32) * MLA_SCALE, -jnp.inf)
        p = jax.nn.softmax(s, axis=-1).astype(v.dtype)
        return jnp.einsum('bhqk,bkhd->bqhd', p, v)

    o = lax.map(block, (qn, qr, jnp.arange(nb)))
    return o.transpose(1, 0, 2, 3, 4).reshape(B, S, H, V_HEAD)


def _softmax_accumulate(carry, s, lat):
    m, den, acc = carry
    m_new = jnp.maximum(m, jnp.max(s, axis=-1))
    corr = jnp.exp(m - m_new)
    p = jnp.exp(s - m_new[..., None])
    acc = acc * corr[..., None] + jnp.einsum('bhtp,bpc->bhtc', p, lat.astype(jnp.float32))
    return (m_new, den * corr + jnp.sum(p, axis=-1), acc)


def _mla_paged(q_nope, q_rope, c_kv, k_rope, cache_lat, cache_rope, page_table, layer, w_uk, w_uv):
    B, T, H, _ = q_nope.shape
    q_abs = jnp.einsum('bthd,chd->bthc', q_nope, w_uk)

    def scores(lat, rk):
        s = jnp.einsum('bthc,bpc->bhtp', q_abs, lat) + jnp.einsum('bthr,bpr->bhtp', q_rope, rk)
        return s.astype(jnp.float32) * MLA_SCALE

    def page_step(carry, phys):
        lat = cache_lat[layer, phys]
        rk = cache_rope[layer, phys]
        return _softmax_accumulate(carry, scores(lat, rk), lat), None

    init = (jnp.full((B, H, T), -jnp.inf, jnp.float32),
            jnp.zeros((B, H, T), jnp.float32),
            jnp.zeros((B, H, T, KV_RANK), jnp.float32))
    carry, _ = lax.scan(page_step, init, page_table.T)
    causal = jnp.tril(jnp.ones((T, T), dtype=bool))
    s_self = jnp.where(causal[None, None], scores(c_kv, k_rope), -jnp.inf)
    _, den, acc = _softmax_accumulate(carry, s_self, c_kv)
    lat_out = (acc / den[..., None]).astype(q_nope.dtype)
    return jnp.einsum('bhtc,chd->bthd', lat_out, w_uv)


def _gla_chunked(q, k, v, g):
    B, S, H, _ = q.shape
    nc = S // GLA_CHUNK

    def chunks(a):
        return a.reshape(B, nc, GLA_CHUNK, H, a.shape[-1]).transpose(1, 0, 3, 2, 4).astype(jnp.float32)

    causal = jnp.tril(jnp.ones((GLA_CHUNK, GLA_CHUNK), dtype=bool))[None, None, :, :, None]

    def step(state, xs):
        qc, kc, vc, gc = xs
        b = jnp.cumsum(gc, axis=2)
        o_inter = jnp.einsum('bhik,bhkv->bhiv', qc * jnp.exp(b), state)
        decay = jnp.exp(jnp.where(causal, b[:, :, :, None, :] - b[:, :, None, :, :], -jnp.inf))
        attn = jnp.einsum('bhik,bhjk,bhijk->bhij', qc, kc, decay)
        o = o_inter + jnp.einsum('bhij,bhjv->bhiv', attn, vc)
        b_last = b[:, :, -1:, :]
        state = (jnp.exp(b_last[:, :, 0, :])[..., None] * state
                 + jnp.einsum('bhjk,bhjv->bhkv', kc * jnp.exp(b_last - b), vc))
        return state, o

    s0 = jnp.zeros((B, H, GLA_DK, GLA_DV), jnp.float32)
    s_fin, o = lax.scan(step, s0, (chunks(q), chunks(k), chunks(v), chunks(g)))
    return o.transpose(1, 0, 3, 2, 4).reshape(B, S, H, GLA_DV), s_fin


def _gla_recurrent(q, k, v, g, state):
    def step(s, xs):
        qt, kt, vt, gt = xs
        s = jnp.exp(gt)[..., None] * s + kt[..., :, None] * vt[..., None, :]
        return s, jnp.einsum('bhk,bhkv->bhv', qt, s)

    xs = tuple(a.transpose(1, 0, 2, 3).astype(jnp.float32) for a in (q, k, v, g))
    s_fin, o = lax.scan(step, state.astype(jnp.float32), xs)
    return o.transpose(1, 0, 2, 3), s_fin


def _gla_heads(gq, gk, gv, ga, gg, w_alpha, b_alpha, gla_norm, state):
    B, T, _ = gq.shape
    shp = (B, T, GLA_HEADS, -1)
    q = gq.reshape(shp) * (GLA_DK ** -0.5)
    k = gk.reshape(shp)
    v = gv.reshape(shp)
    g = (jax.nn.log_sigmoid((ga @ w_alpha + b_alpha).astype(jnp.float32)) / GATE_NORM).reshape(shp)
    if state is None:
        o, s_new = _gla_chunked(q, k, v, g)
    else:
        o, s_new = _gla_recurrent(q, k, v, g, state)
    o = _rmsnorm(o.astype(gq.dtype), gla_norm) * jax.nn.silu(gg.reshape(B, T, GLA_HEADS, GLA_DV))
    return o.reshape(B, T, MIX_GLA), s_new


def _conv_ffn(h, hist, w_up, conv_w, conv_b, w_down):
    T = h.shape[1]
    u = h @ w_up
    u_ext = jnp.concatenate([hist.astype(u.dtype), u], axis=1)
    c = sum((u_ext[:, i:i + T] * conv_w[i] for i in range(CONV_W)), conv_b)
    gate, val = jnp.split(c, 2, axis=-1)
    out = (jax.nn.gelu(gate, approximate=True) * val) @ w_down
    return out, u_ext[:, T:]


def setup_inputs(seed: int = 0) -> dict:
    key = jax.random.key(seed)
    ks = jax.random.split(key, 32)
    n_pages = PAST_LEN // PAGE_SIZE
    used = DEC_BATCH * n_pages
    n_phys = used + max(1, used // 4)
    page_table = jax.random.permutation(ks[0], n_phys)[:used].reshape(DEC_BATCH, n_pages).astype(jnp.int32)

    def nrm(k, shape, fan_in):
        return jax.random.normal(k, shape, jnp.float32) * (fan_in ** -0.5)

    def gain(k, n):
        return 1.0 + 0.05 * jax.random.normal(k, (DEPTH, n), jnp.float32)

    return {
        'x_prompt': jax.random.normal(ks[1], (BATCH, SEQ, D_MODEL), jnp.float32),
        'x_sample': jax.random.normal(ks[2], (DEC_BATCH, DEC_SEQ, D_MODEL), jnp.float32),
        'cache_kv_latent': jax.random.normal(ks[3], (DEPTH, n_phys, PAGE_SIZE, KV_RANK), jnp.float32),
        'cache_k_rope': jax.random.normal(ks[4], (DEPTH, n_phys, PAGE_SIZE, QK_ROPE), jnp.float32),
        'state_gla': jax.random.normal(ks[5], (DEPTH, DEC_BATCH, GLA_HEADS, GLA_DK, GLA_DV), jnp.float32),
        'state_ffn_conv': jax.random.normal(ks[6], (DEPTH, DEC_BATCH, CONV_W - 1, 2 * D_FF), jnp.float32),
        'page_table': page_table,
        'norm_attn_pre': gain(ks[7], D_MODEL),
        'w_in': nrm(ks[8], (DEPTH, D_MODEL, D_IN), D_MODEL),
        'q_norm': gain(ks[9], Q_RANK),
        'w_uq': nrm(ks[10], (DEPTH, Q_RANK, MLA_HEADS * (QK_NOPE + QK_ROPE)), Q_RANK),
        'kv_norm': gain(ks[11], KV_RANK),
        'w_uk': nrm(ks[12], (DEPTH, KV_RANK, MLA_HEADS, QK_NOPE), KV_RANK),
        'w_uv': nrm(ks[13], (DEPTH, KV_RANK, MLA_HEADS, V_HEAD), KV_RANK),
        'w_alpha': nrm(ks[14], (DEPTH, GATE_RANK, GLA_QK), GATE_RANK),
        'b_alpha': 0.1 * jax.random.normal(ks[15], (DEPTH, GLA_QK), jnp.float32),
        'gla_norm': gain(ks[16], GLA_DV),
        'w_o': nrm(ks[17], (DEPTH, D_MIX, D_MODEL), D_MIX),
        'norm_attn_post': gain(ks[18], D_MODEL),
        'norm_ffn_pre': gain(ks[19], D_MODEL),
        'w_up': nrm(ks[20], (DEPTH, D_MODEL, 2 * D_FF), D_MODEL),
        'conv_w': nrm(ks[21], (DEPTH, CONV_W, 2 * D_FF), CONV_W),
        'conv_b': 0.02 * jax.random.normal(ks[22], (DEPTH, 2 * D_FF), jnp.float32),
        'w_down': nrm(ks[23], (DEPTH, D_FF, D_MODEL), D_FF),
        'norm_ffn_post': gain(ks[24], D_MODEL),
    }


def reference(x_prompt, x_sample, cache_kv_latent, cache_k_rope, state_gla, state_ffn_conv, page_table,
              norm_attn_pre, w_in, q_norm, w_uq, kv_norm, w_uk, w_uv, w_alpha, b_alpha, gla_norm, w_o,
              norm_attn_post, norm_ffn_pre, w_up, conv_w, conv_b, w_down, norm_ffn_post):
    S = x_prompt.shape[1]
    T = x_sample.shape[1]
    past = page_table.shape[1] * cache_kv_latent.shape[2]
    pos_p = jnp.arange(S, dtype=jnp.float32)
    pos_s = (past + jnp.arange(T)).astype(jnp.float32)

    def layer(l, x, pos, sample):
        B, L, _ = x.shape
        a = _rmsnorm(x, norm_attn_pre[l])
        q_nope, q_rope, c_kv, k_rope, gla_in = _project(a, pos, w_in[l], q_norm[l], w_uq[l], kv_norm[l])
        if sample:
            o_mla = _mla_paged(q_nope, q_rope, c_kv, k_rope, cache_kv_latent, cache_k_rope, page_table, l,
                               w_uk[l], w_uv[l])
            gla_state = state_gla[l]
            conv_hist = state_ffn_conv[l]
            state_dtype = state_gla.dtype
        else:
            o_mla = _mla_prompt(q_nope, q_rope, c_kv, k_rope, w_uk[l], w_uv[l])
            gla_state = None
            conv_hist = jnp.zeros((B, CONV_W - 1, 2 * D_FF), x.dtype)
            state_dtype = x.dtype
        gq, gk, gv, ga, gg = gla_in
        o_gla, gla_new = _gla_heads(gq, gk, gv, ga, gg, w_alpha[l], b_alpha[l], gla_norm[l], gla_state)
        mix = jnp.concatenate([o_mla.reshape(B, L, MIX_MLA).astype(x.dtype), o_gla], axis=-1) @ w_o[l]
        x = x + _rmsnorm(mix, norm_attn_post[l])
        f, conv_new = _conv_ffn(_rmsnorm(x, norm_ffn_pre[l]), conv_hist, w_up[l], conv_w[l], conv_b[l], w_down[l])
        x = x + _rmsnorm(f, norm_ffn_post[l])
        return x, c_kv, k_rope, gla_new.astype(state_dtype), conv_new

    yp, ys = x_prompt, x_sample
    lat_p, rope_p, gla_p, conv_p = [], [], [], []
    lat_s, rope_s, gla_s, conv_s = [], [], [], []
    for l in range(DEPTH):
        yp, c1, r1, g1, v1 = layer(l, yp, pos_p, False)
        ys, c2, r2, g2, v2 = layer(l, ys, pos_s, True)
        lat_p.append(c1); rope_p.append(r1); gla_p.append(g1); conv_p.append(v1)
        lat_s.append(c2); rope_s.append(r2); gla_s.append(g2); conv_s.append(v2)
    kv_lat_p = jnp.stack(lat_p)
    k_rope_p = jnp.stack(rope_p)
    gla_state_p = jnp.stack(gla_p)
    ffn_conv_p = jnp.stack(conv_p)
    kv_lat_s = jnp.stack(lat_s)
    k_rope_s = jnp.stack(rope_s)
    gla_state_s = jnp.stack(gla_s)
    ffn_conv_s = jnp.stack(conv_s)
    return (yp, ys, kv_lat_p, k_rope_p, gla_state_p, ffn_conv_p, kv_lat_s, k_rope_s, gla_state_s, ffn_conv_s)
```

```python
import functools

import numpy as np
import jax
import jax.numpy as jnp
from jax import lax
from jax.experimental import pallas as pl
from jax.experimental.pallas import tpu as pltpu

EPS = 1e-6
ROPE_BASE = 10000.0
GATE_NORM = 16.0
LANES = 128
V7X_VMEM_LIMIT_BYTES = 56 * 1024 * 1024

F32 = jnp.float32
BF16 = jnp.bfloat16
NEG_INF = float("-inf")


def _cparams(sem):
    return pltpu.CompilerParams(dimension_semantics=sem, vmem_limit_bytes=V7X_VMEM_LIMIT_BYTES)


def _rms(x, w):
    return x * lax.rsqrt(jnp.mean(x * x, axis=-1, keepdims=True) + EPS) * w


def _dot(a, b):
    return jnp.dot(a, b, preferred_element_type=F32)


def _dot_nt(a, b):
    return lax.dot_general(a, b, (((1,), (1,)), ((), ())), preferred_element_type=F32)


def _dot_tn(a, b):
    return lax.dot_general(a, b, (((0,), (0,)), ((), ())), preferred_element_type=F32)


def _const_spec(shape):
    nd = len(shape)
    return pl.BlockSpec(shape, lambda *_: (0,) * nd, pipeline_mode=pl.Buffered(1))


def _proj_kernel(dims, sample, x_ref, ct_ref, st_ref, wn_ref, win_ref, qn_ref, kvn_ref, wq_ref,
                 wa_ref, wb_ref, walpha_ref, balpha_ref, *outs):
    (q_rank, kv_rank, heads, nope, gla_qk, mix_gla, mla_scale, gla_qscale) = dims
    a = _rms(x_ref[...], wn_ref[...]).astype(BF16)

    off = [0]

    def mm(width):
        lo = off[0]
        off[0] += width
        return _dot(a, win_ref[:, lo:lo + width])

    cq = mm(q_rank)
    ckv = mm(kv_rank)
    gq = mm(gla_qk)
    gk = mm(gla_qk)
    gv = mm(mix_gla)
    gg = mm(mix_gla)
    lb = mm(LANES)
    lbs = mm(LANES)

    ct = ct_ref[...]
    st = st_ref[...]

    cqn = _rms(cq, qn_ref[...]).astype(BF16)
    hn = heads * nope
    q_nope = _dot(cqn, wq_ref[:, 0:hn]) * mla_scale
    q_r = _dot(cqn, wq_ref[:, hn:hn + heads * LANES])
    q_rs = _dot(cqn, wq_ref[:, hn + heads * LANES:hn + 2 * heads * LANES])

    c_kv = _rms(ckv, kvn_ref[...])
    c_kv_b = c_kv.astype(BF16)
    k_rope_blk = lb * ct + lbs * st

    glog = _dot(lb.astype(BF16), walpha_ref[...]) + balpha_ref[...]
    g = (jnp.minimum(glog, 0.0) - jnp.log1p(jnp.exp(-jnp.abs(glog)))) * (1.0 / GATE_NORM)

    if sample:
        qabs_ref, qrope_ref, ckv_ref, kr_ref, gq_ref, gk_ref, gv_ref, g_ref, gg_ref = outs
        for h in range(heads):
            qn_h = q_nope[:, h * nope:(h + 1) * nope].astype(BF16)
            qabs_ref[:, h * kv_rank:(h + 1) * kv_rank] = _dot(qn_h, wa_ref[h]).astype(BF16)
            blk = slice(h * LANES, (h + 1) * LANES)
            qrope_ref[:, blk] = ((q_r[:, blk] * ct + q_rs[:, blk] * st) * mla_scale).astype(BF16)
    else:
        q_ref, k_ref, v_ref, ckv_ref, kr_ref, gq_ref, gk_ref, gv_ref, g_ref, gg_ref = outs
        k_nope = _dot(c_kv_b, wa_ref[...])
        v_ref[...] = _dot(c_kv_b, wb_ref[...]).astype(BF16)
        k_rope_b = k_rope_blk.astype(BF16)
        for h in range(heads):
            blk = slice(h * LANES, (h + 1) * LANES)
            base = h * (nope + LANES)
            q_ref[:, base:base + nope] = q_nope[:, h * nope:(h + 1) * nope].astype(BF16)
            q_ref[:, base + nope:base + nope + LANES] = (
                (q_r[:, blk] * ct + q_rs[:, blk] * st) * mla_scale).astype(BF16)
            k_ref[:, base:base + nope] = k_nope[:, h * nope:(h + 1) * nope].astype(BF16)
            k_ref[:, base + nope:base + nope + LANES] = k_rope_b

    ckv_ref[...] = c_kv
    kr_ref[...] = k_rope_blk[:, 0:kr_ref.shape[1]]
    gq_ref[...] = gq * gla_qscale
    gk_ref[...] = gk
    gv_ref[...] = gv
    g_ref[...] = g
    gg_ref[...] = gg


def _proj_call(x, ct, st, wts, dims, rope, sample, tm):
    (q_rank, kv_rank, heads, nope, gla_qk, mix_gla, _, _) = dims
    m, d = x.shape
    row = lambda w: pl.BlockSpec((tm, w), lambda i: (i, 0))
    in_specs = [row(d), row(LANES), row(LANES)] + [_const_spec(w.shape) for w in wts]
    gla_shapes = [(m, gla_qk), (m, gla_qk), (m, mix_gla), (m, gla_qk), (m, mix_gla)]
    if sample:
        out_shapes = [((m, heads * kv_rank), BF16), ((m, heads * LANES), BF16)]
    else:
        out_shapes = [((m, heads * (nope + LANES)), BF16), ((m, heads * (nope + LANES)), BF16),
                      ((m, heads * nope), BF16)]
    out_shapes += [((m, kv_rank), F32), ((m, rope), F32)] + [(s, F32) for s in gla_shapes]
    return pl.pallas_call(
        functools.partial(_proj_kernel, dims, sample),
        grid=(m // tm,),
        in_specs=in_specs,
        out_specs=[row(s[1]) for s, _ in out_shapes],
        out_shape=[jax.ShapeDtypeStruct(s, dt) for s, dt in out_shapes],
        compiler_params=_cparams(("parallel",)),
        name="proj_sample" if sample else "proj_prompt",
    )(x, ct, st, *wts)


def _flash_kernel(q_ref, k_ref, v_ref, o_ref, m_ref, l_ref, acc_ref):
    i = pl.program_id(1)
    j = pl.program_id(2)
    tq, tk = q_ref.shape[0], k_ref.shape[0]

    @pl.when(j == 0)
    def _():
        m_ref[...] = jnp.full(m_ref.shape, NEG_INF, F32)
        l_ref[...] = jnp.zeros(l_ref.shape, F32)
        acc_ref[...] = jnp.zeros(acc_ref.shape, F32)

    def update(masked):
        s = _dot_nt(q_ref[...], k_ref[...])
        if masked:
            rowi = lax.broadcasted_iota(jnp.int32, (tq, tk), 0)
            coli = lax.broadcasted_iota(jnp.int32, (tq, tk), 1)
            s = jnp.where(coli <= rowi, s, NEG_INF)
        m_prev = m_ref[...]
        m_new = jnp.maximum(m_prev, jnp.max(s, axis=-1, keepdims=True))
        alpha = jnp.exp(m_prev - m_new)
        p = jnp.exp(s - m_new)
        l_ref[...] = alpha * l_ref[...] + jnp.sum(p, axis=-1, keepdims=True)
        acc_ref[...] = alpha * acc_ref[...] + _dot(p.astype(BF16), v_ref[...])
        m_ref[...] = m_new

    @pl.when(j < i)
    def _():
        update(False)

    @pl.when(j == i)
    def _():
        update(True)
        o_ref[...] = (acc_ref[...] / l_ref[...]).astype(o_ref.dtype)


def _flash_call(q, k, v, heads, qk_w, v_w, t):
    s = q.shape[0]
    n = s // t
    return pl.pallas_call(
        _flash_kernel,
        grid=(heads, n, n),
        in_specs=[pl.BlockSpec((t, qk_w), lambda h, i, j: (i, h)),
                  pl.BlockSpec((t, qk_w), lambda h, i, j: (jnp.minimum(j, i), h)),
                  pl.BlockSpec((t, v_w), lambda h, i, j: (jnp.minimum(j, i), h))],
        out_specs=pl.BlockSpec((t, v_w), lambda h, i, j: (i, h)),
        out_shape=jax.ShapeDtypeStruct((s, heads * v_w), BF16),
        scratch_shapes=[pltpu.VMEM((t, 1), F32), pltpu.VMEM((t, 1), F32), pltpu.VMEM((t, v_w), F32)],
        compiler_params=_cparams(("parallel", "parallel", "arbitrary")),
        name="mla_prompt_flash",
    )(q, k, v)


def _gla_chunk_kernel(heads, dk, dv, q_ref, k_ref, v_ref, g_ref, gg_ref, nw_ref, o_ref, sfin_ref,
                      state_ref):
    c = pl.program_id(0)
    ch = q_ref.shape[0]

    @pl.when(c == 0)
    def _():
        state_ref[...] = jnp.zeros(state_ref.shape, F32)

    rowi = lax.broadcasted_iota(jnp.int32, (ch, ch), 0)
    coli = lax.broadcasted_iota(jnp.int32, (ch, ch), 1)
    causal = coli <= rowi
    tri = causal.astype(F32)
    nw = nw_ref[...]
    for h in range(heads):
        ks = slice(h * dk, (h + 1) * dk)
        vs = slice(h * dv, (h + 1) * dv)
        b = jnp.dot(tri, g_ref[:, ks], preferred_element_type=F32, precision=lax.Precision.HIGHEST)
        b_last = b[ch - 1:ch, :]
        qb = (q_ref[:, ks] * jnp.exp(b)).astype(BF16)
        kb = (k_ref[:, ks] * jnp.exp(-b)).astype(BF16)
        kl = (k_ref[:, ks] * jnp.exp(b_last - b)).astype(BF16)
        vb = v_ref[:, vs].astype(BF16)
        state = state_ref[h]
        attn = jnp.where(causal, _dot_nt(qb, kb), 0.0)
        o = _dot(qb, state.astype(BF16)) + _dot(attn.astype(BF16), vb)
        decay_col = jnp.transpose(jnp.broadcast_to(jnp.exp(b_last), (dk, dk)))[:, 0:1]
        state_ref[h] = decay_col * state + _dot_tn(kl, vb)
        gate = gg_ref[:, vs]
        o_ref[:, vs] = (_rms(o, nw) * (gate * jax.nn.sigmoid(gate))).astype(o_ref.dtype)

    @pl.when(c == pl.num_programs(0) - 1)
    def _():
        sfin_ref[...] = state_ref[...]


def _gla_chunk_call(gq, gk, gv, g, gg, nw, heads, dk, dv, ch):
    s = gq.shape[0]
    row = lambda w: pl.BlockSpec((ch, w), lambda c: (c, 0))
    return pl.pallas_call(
        functools.partial(_gla_chunk_kernel, heads, dk, dv),
        grid=(s // ch,),
        in_specs=[row(heads * dk), row(heads * dk), row(heads * dv), row(heads * dk), row(heads * dv),
                  pl.BlockSpec((1, dv), lambda c: (0, 0))],
        out_specs=[row(heads * dv), pl.BlockSpec((heads, dk, dv), lambda c: (0, 0, 0))],
        out_shape=[jax.ShapeDtypeStruct((s, heads * dv), BF16),
                   jax.ShapeDtypeStruct((heads, dk, dv), F32)],
        scratch_shapes=[pltpu.VMEM((heads, dk, dv), F32)],
        compiler_params=_cparams(("arbitrary",)),
        name="gla_prompt_chunked",
    )(gq, gk, gv, g, gg, nw)


def _decode_kernel(pages, rope_w, pt_ref, qa_ref, qr_ref, cn_ref, kn_ref, *refs):
    lat_refs = refs[:pages]
    rk_refs = refs[pages:2 * pages]
    o_ref = refs[2 * pages]
    m_ref, l_ref, acc_ref = refs[2 * pages + 1:]
    c = pl.program_id(1)

    @pl.when(c == 0)
    def _():
        m_ref[...] = jnp.full(m_ref.shape, NEG_INF, F32)
        l_ref[...] = jnp.zeros(l_ref.shape, F32)
        acc_ref[...] = jnp.zeros(acc_ref.shape, F32)

    qa = qa_ref[...]
    qr = qr_ref[:, 0:rope_w]
    lats = [r[...].astype(BF16) for r in lat_refs]
    s = jnp.concatenate(
        [_dot_nt(qa, lats[p]) + _dot_nt(qr, rk_refs[p][...].astype(BF16)) for p in range(pages)],
        axis=1)
    m_prev = m_ref[...]
    m_new = jnp.maximum(m_prev, jnp.max(s, axis=-1, keepdims=True))
    alpha = jnp.exp(m_prev - m_new)
    p = jnp.exp(s - m_new)
    pb = p.astype(BF16)
    pg = lats[0].shape[0]
    pv = _dot(pb[:, 0:pg], lats[0])
    for i in range(1, pages):
        pv = pv + _dot(pb[:, i * pg:(i + 1) * pg], lats[i])
    l_ref[...] = alpha * l_ref[...] + jnp.sum(p, axis=-1, keepdims=True)
    acc_ref[...] = alpha * acc_ref[...] + pv
    m_ref[...] = m_new

    @pl.when(c == pl.num_programs(1) - 1)
    def _():
        cn = cn_ref[...]
        kn = kn_ref[...]
        s_self = (jnp.sum(qa.astype(F32) * cn, axis=-1, keepdims=True)
                  + jnp.sum(qr.astype(F32) * kn, axis=-1, keepdims=True))
        m_prev = m_ref[...]
        m_new = jnp.maximum(m_prev, s_self)
        alpha = jnp.exp(m_prev - m_new)
        p_self = jnp.exp(s_self - m_new)
        l_fin = alpha * l_ref[...] + p_self
        acc = alpha * acc_ref[...] + p_self * cn
        o_ref[...] = (acc / l_fin).astype(o_ref.dtype)


def _decode_call(page_table, q_abs, q_rope, ckv_new, kr_new, cache_lat, cache_rope, layer, pages):
    b, heads, kv_rank = q_abs.shape
    n_pages = page_table.shape[1]
    page, rope_w = cache_rope.shape[2], cache_rope.shape[3]

    def lat_spec(p):
        return pl.BlockSpec((None, None, page, kv_rank),
                            lambda bi, c, pt: (layer, pt[bi, c * pages + p], 0, 0))

    def rk_spec(p):
        return pl.BlockSpec((None, None, page, rope_w),
                            lambda bi, c, pt: (layer, pt[bi, c * pages + p], 0, 0))

    per_b = lambda *shape: pl.BlockSpec((None,) + shape, lambda bi, c, pt: (bi,) + (0,) * len(shape))
    grid_spec = pltpu.PrefetchScalarGridSpec(
        num_scalar_prefetch=1,
        grid=(b, n_pages // pages),
        in_specs=[per_b(heads, kv_rank), per_b(heads, LANES), per_b(1, kv_rank), per_b(1, rope_w)]
                 + [lat_spec(p) for p in range(pages)] + [rk_spec(p) for p in range(pages)],
        out_specs=per_b(heads, kv_rank),
        scratch_shapes=[pltpu.VMEM((heads, 1), F32), pltpu.VMEM((heads, 1), F32),
                        pltpu.VMEM((heads, kv_rank), F32)],
    )
    return pl.pallas_call(
        functools.partial(_decode_kernel, pages, rope_w),
        grid_spec=grid_spec,
        out_shape=jax.ShapeDtypeStruct((b, heads, kv_rank), BF16),
        compiler_params=_cparams(("parallel", "arbitrary")),
        name="mla_paged_decode",
    )(page_table, q_abs, q_rope, ckv_new, kr_new, *([cache_lat] * pages), *([cache_rope] * pages))


def _uv_kernel(x_ref, w_ref, o_ref):
    o_ref[...] = _dot(x_ref[...], w_ref[...]).astype(o_ref.dtype)


def _uv_call(lat_out, w_uv_h):
    heads, c, dv = w_uv_h.shape
    b = lat_out.shape[0]
    return pl.pallas_call(
        _uv_kernel,
        grid=(heads,),
        in_specs=[pl.BlockSpec((b, c), lambda h: (0, h)),
                  pl.BlockSpec((None, c, dv), lambda h: (h, 0, 0))],
        out_specs=pl.BlockSpec((b, dv), lambda h: (0, h)),
        out_shape=jax.ShapeDtypeStruct((b, heads * dv), BF16),
        compiler_params=_cparams(("parallel",)),
        name="mla_decode_value_up",
    )(lat_out, w_uv_h)


def _gla_step_kernel(heads, q_ref, k_ref, g_ref, v_ref, gg_ref, nw_ref, s_ref, o_ref, snew_ref):
    dk = q_ref.shape[1]
    pad = jnp.zeros((dk - 3 * heads, dk), F32)
    cols = jnp.transpose(jnp.concatenate([q_ref[...], k_ref[...], jnp.exp(g_ref[...]), pad], axis=0))
    nw = nw_ref[...]
    for h in range(heads):
        q_col = cols[:, h:h + 1]
        k_col = cols[:, heads + h:heads + h + 1]
        e_col = cols[:, 2 * heads + h:2 * heads + h + 1]
        s_new = e_col * s_ref[h] + k_col * v_ref[h:h + 1, :]
        snew_ref[h] = s_new
        o = jnp.sum(q_col * s_new, axis=0, keepdims=True)
        gate = gg_ref[h:h + 1, :]
        o_ref[h:h + 1, :] = (_rms(o, nw) * (gate * jax.nn.sigmoid(gate))).astype(o_ref.dtype)


def _gla_step_call(gq, gk, g, gv, gg, nw, state, layer):
    b, heads, dk = gq.shape
    dv = gv.shape[2]
    per_b = lambda w: pl.BlockSpec((None, heads, w), lambda bi: (bi, 0, 0))
    return pl.pallas_call(
        functools.partial(_gla_step_kernel, heads),
        grid=(b,),
        in_specs=[per_b(dk), per_b(dk), per_b(dk), per_b(dv), per_b(dv),
                  pl.BlockSpec((1, dv), lambda bi: (0, 0)),
                  pl.BlockSpec((None, None, heads, dk, dv), lambda bi: (layer, bi, 0, 0, 0))],
        out_specs=[per_b(dv), pl.BlockSpec((None, heads, dk, dv), lambda bi: (bi, 0, 0, 0))],
        out_shape=[jax.ShapeDtypeStruct((b, heads, dv), BF16),
                   jax.ShapeDtypeStruct((b, heads, dk, dv), F32)],
        compiler_params=_cparams(("parallel",)),
        name="gla_sample_step",
    )(gq, gk, g, gv, gg, nw, state)


def _post_attn_kernel(x_ref, om_ref, og_ref, wo_ref, np_ref, nf_ref, h_ref, a_ref):
    nm = om_ref.shape[1]
    mix = _dot(om_ref[...], wo_ref[0:nm, :]) + _dot(og_ref[...], wo_ref[nm:, :])
    h = x_ref[...] + _rms(mix, np_ref[...])
    h_ref[...] = h
    a_ref[...] = _rms(h, nf_ref[...]).astype(a_ref.dtype)


def _post_attn_call(x, o_mla, o_gla, wo, n_post, n_ffn, tm):
    m, d = x.shape
    row = lambda w: pl.BlockSpec((tm, w), lambda i: (i, 0))
    return pl.pallas_call(
        _post_attn_kernel,
        grid=(m // tm,),
        in_specs=[row(d), row(o_mla.shape[1]), row(o_gla.shape[1]), _const_spec(wo.shape),
                  _const_spec(n_post.shape), _const_spec(n_ffn.shape)],
        out_specs=[row(d), row(d)],
        out_shape=[jax.ShapeDtypeStruct((m, d), F32), jax.ShapeDtypeStruct((m, d), BF16)],
        compiler_params=_cparams(("parallel",)),
        name="attn_out_proj",
    )(x, o_mla, o_gla, wo, n_post, n_ffn)


HALO = 8


def _gelu_tanh(x):
    return 0.5 * x * (1.0 + jnp.tanh(np.sqrt(2.0 / np.pi) * (x + 0.044715 * (x * x * x))))


def _ffn_prompt_kernel(a_ref, ah_ref, h_ref, wg_ref, wv_ref, cwg_ref, cwv_ref, cbg_ref, cbv_ref, wd_ref,
                       nw_ref, y_ref, tg_ref, tv_ref, acc_ref, ug_ref, uv_ref):
    i = pl.program_id(0)
    j = pl.program_id(1)
    tm = a_ref.shape[0]

    @pl.when(j == 0)
    def _():
        acc_ref[...] = jnp.zeros(acc_ref.shape, F32)

    keep = (i > 0).astype(F32)
    a = a_ref[...]
    ah = ah_ref[...]

    def conv_half(w_ref, cw_ref, cb_ref, u_ref, tail_ref):
        u_ref[0:HALO, :] = _dot(ah, w_ref[...]) * keep
        u = _dot(a, w_ref[...])
        u_ref[HALO:, :] = u
        tail_ref[...] = u[tm - HALO:, :]
        return (u_ref[pl.ds(HALO - 2, tm), :] * cw_ref[0:1, :] + u_ref[pl.ds(HALO - 1, tm), :] * cw_ref[1:2, :]
                + u * cw_ref[2:3, :] + cb_ref[...])

    cg = conv_half(wg_ref, cwg_ref, cbg_ref, ug_ref, tg_ref)
    cv = conv_half(wv_ref, cwv_ref, cbv_ref, uv_ref, tv_ref)
    act = (_gelu_tanh(cg) * cv).astype(BF16)
    acc_ref[...] += _dot(act, wd_ref[...])

    @pl.when(j == pl.num_programs(1) - 1)
    def _():
        y_ref[...] = h_ref[...] + _rms(acc_ref[...], nw_ref[...])


def _ffn_prompt_call(a2, h, w_up, conv_w, conv_b, w_down, n_post, tm, tf):
    m, d = h.shape
    dff = w_down.shape[0]
    nf = dff // tf
    hb = tm // HALO
    cb = conv_b.reshape(1, 2 * dff)
    return pl.pallas_call(
        _ffn_prompt_kernel,
        grid=(m // tm, nf),
        in_specs=[pl.BlockSpec((tm, d), lambda i, j: (i, 0)),
                  pl.BlockSpec((HALO, d), lambda i, j: (jnp.maximum(i * hb - 1, 0), 0)),
                  pl.BlockSpec((tm, d), lambda i, j: (i, 0)),
                  pl.BlockSpec((d, tf), lambda i, j: (0, j)),
                  pl.BlockSpec((d, tf), lambda i, j: (0, nf + j)),
                  pl.BlockSpec((conv_w.shape[0], tf), lambda i, j: (0, j)),
                  pl.BlockSpec((conv_w.shape[0], tf), lambda i, j: (0, nf + j)),
                  pl.BlockSpec((1, tf), lambda i, j: (0, j)),
                  pl.BlockSpec((1, tf), lambda i, j: (0, nf + j)),
                  pl.BlockSpec((tf, d), lambda i, j: (j, 0)),
                  pl.BlockSpec((1, d), lambda i, j: (0, 0))],
        out_specs=[pl.BlockSpec((tm, d), lambda i, j: (i, 0)),
                   pl.BlockSpec((HALO, tf), lambda i, j: (i, j)),
                   pl.BlockSpec((HALO, tf), lambda i, j: (i, j))],
        out_shape=[jax.ShapeDtypeStruct((m, d), F32),
                   jax.ShapeDtypeStruct((m // tm * HALO, dff), F32),
                   jax.ShapeDtypeStruct((m // tm * HALO, dff), F32)],
        scratch_shapes=[pltpu.VMEM((tm, d), F32), pltpu.VMEM((tm + HALO, tf), F32),
                        pltpu.VMEM((tm + HALO, tf), F32)],
        compiler_params=_cparams(("arbitrary", "arbitrary")),
        name="ffn_prompt",
    )(a2, a2, h, w_up, w_up, conv_w, conv_w, cb, cb, w_down, n_post)


def _ffn_sample_kernel(a_ref, h_ref, h0g_ref, h1g_ref, h0v_ref, h1v_ref, wg_ref, wv_ref, cwg_ref, cwv_ref,
                       cbg_ref, cbv_ref, wd_ref, nw_ref, y_ref, ug_ref, uv_ref, acc_ref):
    j = pl.program_id(0)

    @pl.when(j == 0)
    def _():
        acc_ref[...] = jnp.zeros(acc_ref.shape, F32)

    a = a_ref[...]

    def conv_half(w_ref, cw_ref, cb_ref, h0_ref, h1_ref, u_ref):
        u = _dot(a, w_ref[...])
        u_ref[...] = u
        return (h0_ref[...] * cw_ref[0:1, :] + h1_ref[...] * cw_ref[1:2, :] + u * cw_ref[2:3, :]
                + cb_ref[...])

    cg = conv_half(wg_ref, cwg_ref, cbg_ref, h0g_ref, h1g_ref, ug_ref)
    cv = conv_half(wv_ref, cwv_ref, cbv_ref, h0v_ref, h1v_ref, uv_ref)
    act = (_gelu_tanh(cg) * cv).astype(BF16)
    acc_ref[...] += _dot(act, wd_ref[...])

    @pl.when(j == pl.num_programs(0) - 1)
    def _():
        y_ref[...] = h_ref[...] + _rms(acc_ref[...], nw_ref[...])


def _ffn_sample_call(a2, h, hist, w_up, conv_w, conv_b, w_down, n_post, tf):
    b, d = h.shape
    dff = w_down.shape[0]
    nf = dff // tf
    cb = conv_b.reshape(1, 2 * dff)
    hist2 = hist.reshape(b, 2 * 2 * dff)
    nb = 2 * nf
    full = lambda w: pl.BlockSpec((b, w), lambda j: (0, 0))
    col = lambda base: pl.BlockSpec((b, tf), lambda j: (0, base + j))
    return pl.pallas_call(
        _ffn_sample_kernel,
        grid=(nf,),
        in_specs=[full(d), full(d), col(0), col(nb), col(nf), col(nb + nf),
                  pl.BlockSpec((d, tf), lambda j: (0, j)),
                  pl.BlockSpec((d, tf), lambda j: (0, nf + j)),
                  pl.BlockSpec((conv_w.shape[0], tf), lambda j: (0, j)),
                  pl.BlockSpec((conv_w.shape[0], tf), lambda j: (0, nf + j)),
                  pl.BlockSpec((1, tf), lambda j: (0, j)),
                  pl.BlockSpec((1, tf), lambda j: (0, nf + j)),
                  pl.BlockSpec((tf, d), lambda j: (j, 0)),
                  pl.BlockSpec((1, d), lambda j: (0, 0))],
        out_specs=[full(d), col(0), col(0)],
        out_shape=[jax.ShapeDtypeStruct((b, d), F32), jax.ShapeDtypeStruct((b, dff), F32),
                   jax.ShapeDtypeStruct((b, dff), F32)],
        scratch_shapes=[pltpu.VMEM((b, d), F32)],
        compiler_params=_cparams(("arbitrary",)),
        name="ffn_sample",
    )(a2, h, hist2, hist2, hist2, hist2, w_up, w_up, conv_w, conv_w, cb, cb, w_down, n_post)


def _rope_tables(pos, half):
    inv = ROPE_BASE ** (-2.0 * jnp.arange(half, dtype=F32) / (2 * half))
    ang = pos[:, None] * inv[None, :]
    zeros = jnp.zeros((pos.shape[0], LANES - 2 * half), F32)
    cos, sin = jnp.cos(ang), jnp.sin(ang)
    return (jnp.concatenate([cos, cos, zeros], axis=1), jnp.concatenate([sin, sin, zeros], axis=1))


def _tile(n, pref):
    return pref if n % pref == 0 else n


def kernel(x_prompt, x_sample, cache_kv_latent, cache_k_rope, state_gla, state_ffn_conv, page_table,
           norm_attn_pre, w_in, q_norm, w_uq, kv_norm, w_uk, w_uv, w_alpha, b_alpha, gla_norm, w_o,
           norm_attn_post, norm_ffn_pre, w_up, conv_w, conv_b, w_down, norm_ffn_post):
    depth = w_in.shape[0]
    _, seq, d = x_prompt.shape
    bs, t_new, _ = x_sample.shape
    assert x_prompt.shape[0] == 1 and t_new == 1
    q_rank = w_uq.shape[1]
    kv_rank, heads, nope = w_uk.shape[1:]
    v_head = w_uv.shape[3]
    rope = cache_k_rope.shape[3]
    half = rope // 2
    assert w_uq.shape[2] == heads * (nope + rope) and nope == LANES and v_head == LANES and rope <= LANES // 2
    _, _, gla_heads, dk, dv = state_gla.shape
    gla_qk, mix_gla = gla_heads * dk, gla_heads * dv
    gate_rank = w_alpha.shape[1]
    assert rope + gate_rank <= LANES
    dff = w_down.shape[1]
    assert conv_w.shape[1] == 3
    past = page_table.shape[1] * cache_kv_latent.shape[2]
    mla_scale = float((nope + rope) ** -0.5)
    dims = (q_rank, kv_rank, heads, nope, gla_qk, mix_gla, mla_scale, float(dk ** -0.5))

    ct_p, st_p = _rope_tables(jnp.arange(seq, dtype=F32), half)
    ct_s, st_s = _rope_tables(jnp.full((bs,), past, F32), half)

    yp, ys = x_prompt[0], x_sample[:, 0]
    outs = {k: [] for k in ("lat_p", "rope_p", "gla_p", "conv_p", "lat_s", "rope_s", "gla_s", "conv_s")}
    for l in range(depth):
        sizes = (q_rank, kv_rank, rope, gla_qk, gla_qk, mix_gla, gate_rank, mix_gla)
        idx = np.cumsum(sizes)[:-1].tolist()
        w_cq, w_ckv, w_kr, w_gq, w_gk, w_gv, w_ga, w_gg = jnp.split(w_in[l], idx, axis=1)
        zc = lambda n: jnp.zeros((d, n), F32)
        lb = jnp.concatenate([w_kr, w_ga, zc(LANES - rope - gate_rank)], axis=1)
        lbs = jnp.concatenate([-w_kr[:, half:], w_kr[:, :half], zc(LANES - rope)], axis=1)
        win = jnp.concatenate([w_cq, w_ckv, w_gq, w_gk, w_gv, w_gg, lb, lbs], axis=1).astype(BF16)
        wq3 = w_uq[l].reshape(q_rank, heads, nope + rope)
        wq_n = wq3[:, :, :nope].reshape(q_rank, heads * nope)
        x1, x2 = wq3[:, :, nope:nope + half], wq3[:, :, nope + half:]
        zq = jnp.zeros((q_rank, heads, LANES - rope), F32)
        wq_r = jnp.concatenate([x1, x2, zq], axis=2).reshape(q_rank, heads * LANES)
        wq_rs = jnp.concatenate([-x2, x1, zq], axis=2).reshape(q_rank, heads * LANES)
        wq = jnp.concatenate([wq_n, wq_r, wq_rs], axis=1).astype(BF16)
        w_uk2 = w_uk[l].reshape(kv_rank, heads * nope).astype(BF16)
        w_uv2 = w_uv[l].reshape(kv_rank, heads * v_head).astype(BF16)
        w_ukT = jnp.transpose(w_uk[l], (1, 2, 0)).astype(BF16)
        w_uv_h = jnp.transpose(w_uv[l], (1, 0, 2)).astype(BF16)
        walpha = jnp.zeros((LANES, gla_qk), F32).at[rope:rope + gate_rank].set(w_alpha[l]).astype(BF16)
        row = lambda v: v.reshape(1, -1)
        common = (row(norm_attn_pre[l]), win, row(q_norm[l]), row(kv_norm[l]), wq)
        tail = (walpha, row(b_alpha[l]))
        wo_b = w_o[l].astype(BF16)
        w_up_b = w_up[l].astype(BF16)
        w_down_b = w_down[l].astype(BF16)
        gnw = row(gla_norm[l])

        tm = _tile(seq, 256)
        (q, k, v, ckv_p, kr_p, gq, gk, gv, g, gg) = _proj_call(
            yp, ct_p, st_p, common + (w_uk2, w_uv2) + tail, dims, rope, False, tm)
        o_mla = _flash_call(q, k, v, heads, nope + LANES, v_head, _tile(seq, 512))
        o_gla, gla_new_p = _gla_chunk_call(gq, gk, gv, g, gg, gnw, gla_heads, dk, dv, _tile(seq, 64))
        h_p, a2_p = _post_attn_call(yp, o_mla, o_gla, wo_b, row(norm_attn_post[l]), row(norm_ffn_pre[l]),
                                    _tile(seq, 512))
        yp, tail_g, tail_v = _ffn_prompt_call(a2_p, h_p, w_up_b, conv_w[l], conv_b[l], w_down_b,
                                              row(norm_ffn_post[l]), _tile(seq, 512), _tile(dff, 512))
        conv_new_p = jnp.concatenate([tail_g[-2:], tail_v[-2:]], axis=1)[None]

        (q_abs, q_rope, ckv_s, kr_s, sq, sk, sv, sg, sgg) = _proj_call(
            ys, ct_s, st_s, common + (w_ukT, w_uv2) + tail, dims, rope, True, bs)
        lat_out = _decode_call(page_table, q_abs.reshape(bs, heads, kv_rank), q_rope.reshape(bs, heads, LANES),
                               ckv_s.reshape(bs, 1, kv_rank), kr_s.reshape(bs, 1, rope),
                               cache_kv_latent, cache_k_rope, l, 8)
        o_mla_s = _uv_call(lat_out.reshape(bs, heads * kv_rank), w_uv_h)
        sh = lambda z, w: z.reshape(bs, gla_heads, w)
        o_gla_s, gla_new_s = _gla_step_call(sh(sq, dk), sh(sk, dk), sh(sg, dk), sh(sv, dv), sh(sgg, dv), gnw,
                                            state_gla, l)
        h_s, a2_s = _post_attn_call(ys, o_mla_s, o_gla_s.reshape(bs, mix_gla), wo_b, row(norm_attn_post[l]),
                                    row(norm_ffn_pre[l]), bs)
        ys, u_g, u_v = _ffn_sample_call(a2_s, h_s, state_ffn_conv[l], w_up_b, conv_w[l], conv_b[l], w_down_b,
                                        row(norm_ffn_post[l]), _tile(dff, 512))
        conv_new_s = jnp.stack([state_ffn_conv[l][:, 1], jnp.concatenate([u_g, u_v], axis=1)], axis=1)

        outs["lat_p"].append(ckv_p[None]); outs["rope_p"].append(kr_p[None])
        outs["gla_p"].append(gla_new_p[None].astype(x_prompt.dtype)); outs["conv_p"].append(conv_new_p)
        outs["lat_s"].append(ckv_s[:, None]); outs["rope_s"].append(kr_s[:, None])
        outs["gla_s"].append(gla_new_s.astype(state_gla.dtype)); outs["conv_s"].append(conv_new_s)

    st = lambda name: jnp.stack(outs[name])
    return (yp[None], ys[:, None], st("lat_p"), st("rope_p"), st("gla_p"), st("conv_p"),
            st("lat_s"), st("rope_s"), st("gla_s"), st("conv_s"))
```

```python
import functools

import numpy as np
import jax
import jax.numpy as jnp
from jax import lax
from jax.experimental import pallas as pl
from jax.experimental.pallas import tpu as pltpu

EPS = 1e-6
ROPE_BASE = 10000.0
GATE_NORM = 16.0
LANES = 128
V7X_VMEM_LIMIT_BYTES = 56 * 1024 * 1024

F32 = jnp.float32
BF16 = jnp.bfloat16
NEG_INF = float("-inf")


def _cparams(sem):
    return pltpu.CompilerParams(dimension_semantics=sem, vmem_limit_bytes=V7X_VMEM_LIMIT_BYTES)


def _rms(x, w):
    return x * lax.rsqrt(jnp.mean(x * x, axis=-1, keepdims=True) + EPS) * w


def _dot(a, b):
    return jnp.dot(a, b, preferred_element_type=F32)


def _dot_nt(a, b):
    return lax.dot_general(a, b, (((1,), (1,)), ((), ())), preferred_element_type=F32)


def _dot_tn(a, b):
    return lax.dot_general(a, b, (((0,), (0,)), ((), ())), preferred_element_type=F32)


def _const_spec(shape):
    nd = len(shape)
    return pl.BlockSpec(shape, lambda *_: (0,) * nd, pipeline_mode=pl.Buffered(1))


def _proj_kernel(dims, sample, x_ref, ct_ref, st_ref, wn_ref, win_ref, qn_ref, kvn_ref, wq_ref,
                 wa_ref, wb_ref, walpha_ref, balpha_ref, *outs):
    (q_rank, kv_rank, heads, nope, gla_qk, mix_gla, mla_scale, gla_qscale) = dims
    a = _rms(x_ref[...], wn_ref[...]).astype(BF16)

    off = [0]

    def mm(width):
        lo = off[0]
        off[0] += width
        return _dot(a, win_ref[:, lo:lo + width])

    cq = mm(q_rank)
    ckv = mm(kv_rank)
    gq = mm(gla_qk)
    gk = mm(gla_qk)
    gv = mm(mix_gla)
    gg = mm(mix_gla)
    lb = mm(LANES)
    lbs = mm(LANES)

    ct = ct_ref[...]
    st = st_ref[...]

    cqn = _rms(cq, qn_ref[...]).astype(BF16)
    hn = heads * nope
    q_nope = _dot(cqn, wq_ref[:, 0:hn]) * mla_scale
    q_r = _dot(cqn, wq_ref[:, hn:hn + heads * LANES])
    q_rs = _dot(cqn, wq_ref[:, hn + heads * LANES:hn + 2 * heads * LANES])

    c_kv = _rms(ckv, kvn_ref[...])
    c_kv_b = c_kv.astype(BF16)
    k_rope_blk = lb * ct + lbs * st

    glog = _dot(lb.astype(BF16), walpha_ref[...]) + balpha_ref[...]
    g = (jnp.minimum(glog, 0.0) - jnp.log1p(jnp.exp(-jnp.abs(glog)))) * (1.0 / GATE_NORM)

    if sample:
        qabs_ref, qrope_ref, ckv_ref, kr_ref, gq_ref, gk_ref, gv_ref, g_ref, gg_ref = outs
        for h in range(heads):
            qn_h = q_nope[:, h * nope:(h + 1) * nope].astype(BF16)
            qabs_ref[:, h * kv_rank:(h + 1) * kv_rank] = _dot(qn_h, wa_ref[h]).astype(BF16)
            blk = slice(h * LANES, (h + 1) * LANES)
            qrope_ref[:, blk] = ((q_r[:, blk] * ct + q_rs[:, blk] * st) * mla_scale).astype(BF16)
    else:
        q_ref, k_ref, v_ref, ckv_ref, kr_ref, gq_ref, gk_ref, gv_ref, g_ref, gg_ref = outs
        k_nope = _dot(c_kv_b, wa_ref[...])
        v_ref[...] = _dot(c_kv_b, wb_ref[...]).astype(BF16)
        k_rope_b = k_rope_blk.astype(BF16)
        for h in range(heads):
            blk = slice(h * LANES, (h + 1) * LANES)
            base = h * (nope + LANES)
            q_ref[:, base:base + nope] = q_nope[:, h * nope:(h + 1) * nope].astype(BF16)
            q_ref[:, base + nope:base + nope + LANES] = (
                (q_r[:, blk] * ct + q_rs[:, blk] * st) * mla_scale).astype(BF16)
            k_ref[:, base:base + nope] = k_nope[:, h * nope:(h + 1) * nope].astype(BF16)
            k_ref[:, base + nope:base + nope + LANES] = k_rope_b

    ckv_ref[...] = c_kv
    kr_ref[...] = k_rope_blk[:, 0:kr_ref.shape[1]]
    gq_ref[...] = gq * gla_qscale
    gk_ref[...] = gk
    gv_ref[...] = gv
    g_ref[...] = g
    gg_ref[...] = gg


def _proj_call(x, ct, st, wts, dims, rope, sample, tm):
    (q_rank, kv_rank, heads, nope, gla_qk, mix_gla, _, _) = dims
    m, d = x.shape
    row = lambda w: pl.BlockSpec((tm, w), lambda i: (i, 0))
    in_specs = [row(d), row(LANES), row(LANES)] + [_const_spec(w.shape) for w in wts]
    gla_shapes = [(m, gla_qk), (m, gla_qk), (m, mix_gla), (m, gla_qk), (m, mix_gla)]
    if sample:
        out_shapes = [((m, heads * kv_rank), BF16), ((m, heads * LANES), BF16)]
    else:
        out_shapes = [((m, heads * (nope + LANES)), BF16), ((m, heads * (nope + LANES)), BF16),
                      ((m, heads * nope), BF16)]
    out_shapes += [((m, kv_rank), F32), ((m, rope), F32)] + [(s, F32) for s in gla_shapes]
    return pl.pallas_call(
        functools.partial(_proj_kernel, dims, sample),
        grid=(m // tm,),
        in_specs=in_specs,
        out_specs=[row(s[1]) for s, _ in out_shapes],
        out_shape=[jax.ShapeDtypeStruct(s, dt) for s, dt in out_shapes],
        compiler_params=_cparams(("parallel",)),
        name="proj_sample" if sample else "proj_prompt",
    )(x, ct, st, *wts)


def _flash_kernel(qi_ref, kj_ref, q_ref, k_ref, v_ref, o_ref, m_ref, l_ref, acc_ref):
    t = pl.program_id(1)
    i = qi_ref[t]
    j = kj_ref[t]
    tq, tk = q_ref.shape[0], k_ref.shape[0]

    @pl.when(j == 0)
    def _():
        m_ref[...] = jnp.full(m_ref.shape, NEG_INF, F32)
        l_ref[...] = jnp.zeros(l_ref.shape, F32)
        acc_ref[...] = jnp.zeros(acc_ref.shape, F32)

    def update(masked):
        s = _dot_nt(q_ref[...], k_ref[...])
        if masked:
            rowi = lax.broadcasted_iota(jnp.int32, (tq, tk), 0)
            coli = lax.broadcasted_iota(jnp.int32, (tq, tk), 1)
            s = jnp.where(coli <= rowi, s, NEG_INF)
        m_prev = m_ref[...]
        m_new = jnp.maximum(m_prev, jnp.max(s, axis=-1, keepdims=True))
        alpha = jnp.exp(m_prev - m_new)
        p = jnp.exp(s - jnp.tile(m_new, (1, tk // LANES)))
        l_ref[...] = alpha * l_ref[...] + jnp.sum(p, axis=-1, keepdims=True)
        acc_ref[...] = alpha * acc_ref[...] + _dot(p.astype(BF16), v_ref[...])
        m_ref[...] = m_new

    @pl.when(j < i)
    def _():
        update(False)

    @pl.when(j == i)
    def _():
        update(True)
        o_ref[...] = (acc_ref[...] / l_ref[...]).astype(o_ref.dtype)


def _flash_call(q, k, v, heads, qk_w, v_w, t):
    assert v_w == LANES
    s = q.shape[0]
    n = s // t
    pairs = [(i, j) for i in range(n) for j in range(i + 1)]
    qi = jnp.asarray([p[0] for p in pairs], jnp.int32)
    kj = jnp.asarray([p[1] for p in pairs], jnp.int32)
    grid_spec = pltpu.PrefetchScalarGridSpec(
        num_scalar_prefetch=2,
        grid=(heads, len(pairs)),
        in_specs=[pl.BlockSpec((t, qk_w), lambda h, p, qi, kj: (qi[p], h)),
                  pl.BlockSpec((t, qk_w), lambda h, p, qi, kj: (kj[p], h)),
                  pl.BlockSpec((t, v_w), lambda h, p, qi, kj: (kj[p], h))],
        out_specs=pl.BlockSpec((t, v_w), lambda h, p, qi, kj: (qi[p], h)),
        scratch_shapes=[pltpu.VMEM((t, LANES), F32), pltpu.VMEM((t, LANES), F32),
                        pltpu.VMEM((t, v_w), F32)],
    )
    return pl.pallas_call(
        _flash_kernel,
        grid_spec=grid_spec,
        out_shape=jax.ShapeDtypeStruct((s, heads * v_w), BF16),
        compiler_params=_cparams(("parallel", "arbitrary")),
        name="mla_prompt_flash",
    )(qi, kj, q, k, v)


def _gla_chunk_kernel(heads, dk, dv, q_ref, k_ref, v_ref, g_ref, gg_ref, nw_ref, o_ref, sfin_ref,
                      state_ref):
    c = pl.program_id(0)
    ch = q_ref.shape[0]

    @pl.when(c == 0)
    def _():
        state_ref[...] = jnp.zeros(state_ref.shape, F32)

    rowi = lax.broadcasted_iota(jnp.int32, (ch, ch), 0)
    coli = lax.broadcasted_iota(jnp.int32, (ch, ch), 1)
    causal = coli <= rowi
    tri = causal.astype(F32)
    nw = nw_ref[...]
    for h in range(heads):
        ks = slice(h * dk, (h + 1) * dk)
        vs = slice(h * dv, (h + 1) * dv)
        b = jnp.dot(tri, g_ref[:, ks], preferred_element_type=F32, precision=lax.Precision.HIGHEST)
        b_last = b[ch - 1:ch, :]
        qb = (q_ref[:, ks] * jnp.exp(b)).astype(BF16)
        kb = (k_ref[:, ks] * jnp.exp(-b)).astype(BF16)
        kl = (k_ref[:, ks] * jnp.exp(b_last - b)).astype(BF16)
        vb = v_ref[:, vs].astype(BF16)
        state = state_ref[h]
        attn = jnp.where(causal, _dot_nt(qb, kb), 0.0)
        o = _dot(qb, state.astype(BF16)) + _dot(attn.astype(BF16), vb)
        decay_col = jnp.transpose(jnp.broadcast_to(jnp.exp(b_last), (dk, dk)))[:, 0:1]
        state_ref[h] = decay_col * state + _dot_tn(kl, vb)
        gate = gg_ref[:, vs]
        o_ref[:, vs] = (_rms(o, nw) * (gate * jax.nn.sigmoid(gate))).astype(o_ref.dtype)

    @pl.when(c == pl.num_programs(0) - 1)
    def _():
        sfin_ref[...] = state_ref[...]


def _gla_chunk_call(gq, gk, gv, g, gg, nw, heads, dk, dv, ch):
    s = gq.shape[0]
    row = lambda w: pl.BlockSpec((ch, w), lambda c: (c, 0))
    return pl.pallas_call(
        functools.partial(_gla_chunk_kernel, heads, dk, dv),
        grid=(s // ch,),
        in_specs=[row(heads * dk), row(heads * dk), row(heads * dv), row(heads * dk), row(heads * dv),
                  pl.BlockSpec((1, dv), lambda c: (0, 0))],
        out_specs=[row(heads * dv), pl.BlockSpec((heads, dk, dv), lambda c: (0, 0, 0))],
        out_shape=[jax.ShapeDtypeStruct((s, heads * dv), BF16),
                   jax.ShapeDtypeStruct((heads, dk, dv), F32)],
        scratch_shapes=[pltpu.VMEM((heads, dk, dv), F32)],
        compiler_params=_cparams(("arbitrary",)),
        name="gla_prompt_chunked",
    )(gq, gk, gv, g, gg, nw)


DECODE_SLOTS = 4
DECODE_AHEAD = 2


def _decode_kernel(pages, n_chunks, layer, pt_ref, qt_ref, qa_ref, qr_ref, cn_ref, kn_ref, lat_hbm, rk_hbm,
                   o_ref, lat_buf, rk_buf, sem):
    b = pl.program_id(0)
    total = pl.num_programs(0) * n_chunks
    heads, kv_rank = o_ref.shape
    pg = lat_hbm.shape[2]
    rope_w = rk_hbm.shape[2]
    reps = kv_rank // LANES

    def chunk_copies(bi, ci, slot):
        cps = []
        for p in range(pages):
            phys = pt_ref[bi, ci * pages + p]
            cps.append(pltpu.make_async_copy(lat_hbm.at[layer, phys], lat_buf.at[slot, pl.ds(p * pg, pg)],
                                             sem.at[0, slot]))
            cps.append(pltpu.make_async_copy(rk_hbm.at[layer, phys], rk_buf.at[slot, p], sem.at[1, slot]))
        return cps

    @pl.when(b == 0)
    def _():
        for c in range(DECODE_AHEAD):
            for cp in chunk_copies(0, c, c % DECODE_SLOTS):
                cp.start()

    qt = qt_ref[...].astype(F32)
    qr = qr_ref[:, 0:rope_w].astype(F32)
    half = pages * pg // 2

    def latent_scores(c):
        slot = c % DECODE_SLOTS
        for cp in chunk_copies(b, c, slot):
            cp.wait()
        return (_dot(lat_buf[slot, 0:half], qt), _dot(lat_buf[slot, half:], qt))

    m = jnp.full((heads, LANES), NEG_INF, F32)
    l = jnp.zeros((heads, LANES), F32)
    acc = jnp.zeros((heads, kv_rank), F32)
    st = latent_scores(0)
    for c in range(n_chunks):
        slot = c % DECODE_SLOTS
        ahead = c + DECODE_AHEAD
        nb, nc = (b, ahead) if ahead < n_chunks else (b + 1, ahead - n_chunks)

        @pl.when(b * n_chunks + ahead < total)
        def _():
            for cp in chunk_copies(nb, nc, nc % DECODE_SLOTS):
                cp.start()

        st_next = latent_scores(c + 1) if c + 1 < n_chunks else None
        per_half = pages // 2
        s = jnp.concatenate(
            [jnp.transpose(st[p // per_half][(p % per_half) * pg:(p % per_half + 1) * pg, :])[0:heads, :]
             + _dot(qr, rk_buf[slot, p]) for p in range(pages)], axis=1)
        m_new = jnp.maximum(m, jnp.max(s, axis=-1, keepdims=True))
        alpha = jnp.exp(m - m_new)
        p_un = jnp.exp(s - jnp.tile(m_new, (1, pages * pg // LANES)))
        l = alpha * l + jnp.sum(p_un, axis=-1, keepdims=True)
        acc = jnp.tile(alpha, (1, reps)) * acc + _dot(p_un, lat_buf[slot])
        m = m_new
        st = st_next

    cn = cn_ref[...]
    kn = kn_ref[...]
    s_self = (jnp.sum(qa_ref[...].astype(F32) * cn, axis=-1, keepdims=True)
              + jnp.sum(qr * kn, axis=-1, keepdims=True))
    m_new = jnp.maximum(m, s_self)
    alpha = jnp.exp(m - m_new)
    p_self = jnp.exp(s_self - m_new)
    l_fin = alpha * l + p_self
    acc_fin = jnp.tile(alpha, (1, reps)) * acc + jnp.tile(p_self, (1, reps)) * cn
    o_ref[...] = (acc_fin / jnp.tile(l_fin, (1, reps))).astype(o_ref.dtype)


def _decode_call(page_table, q_t, q_abs, q_rope, ckv_new, kr_new, cache_lat, cache_rope_t, layer, pages):
    b, heads, kv_rank = q_abs.shape
    n_pages = page_table.shape[1]
    page = cache_lat.shape[2]
    rope_w = cache_rope_t.shape[2]
    n_chunks = n_pages // pages
    assert n_chunks % DECODE_SLOTS == 0 and DECODE_AHEAD < DECODE_SLOTS - 1 and pages % 2 == 0

    per_b = lambda *shape: pl.BlockSpec((None,) + shape, lambda bi, pt: (bi,) + (0,) * len(shape))
    grid_spec = pltpu.PrefetchScalarGridSpec(
        num_scalar_prefetch=1,
        grid=(b,),
        in_specs=[per_b(kv_rank, LANES), per_b(heads, kv_rank), per_b(heads, LANES), per_b(1, kv_rank),
                  per_b(1, rope_w), pl.BlockSpec(memory_space=pl.ANY), pl.BlockSpec(memory_space=pl.ANY)],
        out_specs=per_b(heads, kv_rank),
        scratch_shapes=[pltpu.VMEM((DECODE_SLOTS, pages * page, kv_rank), F32),
                        pltpu.VMEM((DECODE_SLOTS, pages, rope_w, page), F32),
                        pltpu.SemaphoreType.DMA((2, DECODE_SLOTS))],
    )
    return pl.pallas_call(
        functools.partial(_decode_kernel, pages, n_chunks, layer),
        grid_spec=grid_spec,
        out_shape=jax.ShapeDtypeStruct((b, heads, kv_rank), BF16),
        compiler_params=_cparams(("arbitrary",)),
        name="mla_paged_decode",
    )(page_table, q_t, q_abs, q_rope, ckv_new, kr_new, cache_lat, cache_rope_t)


def _uv_kernel(x_ref, w_ref, o_ref):
    o_ref[...] = _dot(x_ref[...], w_ref[...]).astype(o_ref.dtype)


def _uv_call(lat_out, w_uv_h):
    heads, c, dv = w_uv_h.shape
    b = lat_out.shape[0]
    return pl.pallas_call(
        _uv_kernel,
        grid=(heads,),
        in_specs=[pl.BlockSpec((b, c), lambda h: (0, h)),
                  pl.BlockSpec((None, c, dv), lambda h: (h, 0, 0))],
        out_specs=pl.BlockSpec((b, dv), lambda h: (0, h)),
        out_shape=jax.ShapeDtypeStruct((b, heads * dv), BF16),
        compiler_params=_cparams(("parallel",)),
        name="mla_decode_value_up",
    )(lat_out, w_uv_h)


def _gla_step_kernel(heads, q_ref, k_ref, g_ref, v_ref, gg_ref, nw_ref, s_ref, o_ref, snew_ref):
    bb, _, dk = q_ref.shape
    rows = [r[i] for i in range(bb) for r in (q_ref, k_ref)] + [jnp.exp(g_ref[i]) for i in range(bb)]
    pad = jnp.zeros((dk - 3 * heads * bb, dk), F32)
    cols = jnp.transpose(jnp.concatenate(rows + [pad], axis=0))
    nw = nw_ref[...]
    for i in range(bb):
        for h in range(heads):
            cq, ck, ce = (2 * i) * heads + h, (2 * i + 1) * heads + h, (2 * bb + i) * heads + h
            s_new = cols[:, ce:ce + 1] * s_ref[i, h] + cols[:, ck:ck + 1] * v_ref[i, h:h + 1, :]
            snew_ref[i, h] = s_new
            o = jnp.sum(cols[:, cq:cq + 1] * s_new, axis=0, keepdims=True)
            gate = gg_ref[i, h:h + 1, :]
            o_ref[i, h:h + 1, :] = (_rms(o, nw) * (gate * jax.nn.sigmoid(gate))).astype(o_ref.dtype)


def _gla_step_call(gq, gk, g, gv, gg, nw, state, layer):
    b, heads, dk = gq.shape
    dv = gv.shape[2]
    bb = next(n for n in (8, 4, 2, 1) if b % n == 0 and 3 * heads * n <= dk)
    per_b = lambda w: pl.BlockSpec((bb, heads, w), lambda bi: (bi, 0, 0))
    return pl.pallas_call(
        functools.partial(_gla_step_kernel, heads),
        grid=(b // bb,),
        in_specs=[per_b(dk), per_b(dk), per_b(dk), per_b(dv), per_b(dv),
                  pl.BlockSpec((1, dv), lambda bi: (0, 0)),
                  pl.BlockSpec((None, bb, heads, dk, dv), lambda bi: (layer, bi, 0, 0, 0))],
        out_specs=[per_b(dv), pl.BlockSpec((bb, heads, dk, dv), lambda bi: (bi, 0, 0, 0))],
        out_shape=[jax.ShapeDtypeStruct((b, heads, dv), BF16),
                   jax.ShapeDtypeStruct((b, heads, dk, dv), F32)],
        compiler_params=_cparams(("parallel",)),
        name="gla_sample_step",
    )(gq, gk, g, gv, gg, nw, state)


def _post_attn_kernel(x_ref, om_ref, og_ref, wo_ref, np_ref, nf_ref, h_ref, a_ref):
    nm = om_ref.shape[1]
    mix = _dot(om_ref[...], wo_ref[0:nm, :]) + _dot(og_ref[...], wo_ref[nm:, :])
    h = x_ref[...] + _rms(mix, np_ref[...])
    h_ref[...] = h
    a_ref[...] = _rms(h, nf_ref[...]).astype(a_ref.dtype)


def _post_attn_call(x, o_mla, o_gla, wo, n_post, n_ffn, tm):
    m, d = x.shape
    row = lambda w: pl.BlockSpec((tm, w), lambda i: (i, 0))
    return pl.pallas_call(
        _post_attn_kernel,
        grid=(m // tm,),
        in_specs=[row(d), row(o_mla.shape[1]), row(o_gla.shape[1]), _const_spec(wo.shape),
                  _const_spec(n_post.shape), _const_spec(n_ffn.shape)],
        out_specs=[row(d), row(d)],
        out_shape=[jax.ShapeDtypeStruct((m, d), F32), jax.ShapeDtypeStruct((m, d), BF16)],
        compiler_params=_cparams(("parallel",)),
        name="attn_out_proj",
    )(x, o_mla, o_gla, wo, n_post, n_ffn)


HALO = 8

def _gelu_tanh(x):
    return 0.5 * x * (1.0 + jnp.tanh(np.sqrt(2.0 / np.pi) * (x + 0.044715 * (x * x * x))))


def _ffn_prompt_kernel(a_ref, h_ref, wg_ref, wv_ref, cwg_ref, cwv_ref, cbg_ref, cbv_ref, wd_ref,
                       nw_ref, y_ref, tg_ref, tv_ref, acc_ref, ug_ref, uv_ref, pg_ref, pv_ref):
    i = pl.program_id(0)
    j = pl.program_id(1)
    tm = a_ref.shape[0]

    @pl.when(j == 0)
    def _():
        acc_ref[...] = jnp.zeros(acc_ref.shape, F32)

    @pl.when(i == 0)
    def _():
        pg_ref[j] = jnp.zeros(pg_ref.shape[1:], F32)
        pv_ref[j] = jnp.zeros(pv_ref.shape[1:], F32)

    a = a_ref[...]

    def conv_half(w_ref, cw_ref, cb_ref, u_ref, tail_ref, prev_ref):
        u_ref[0:HALO, :] = prev_ref[j]
        u = _dot(a, w_ref[...])
        u_ref[HALO:, :] = u
        tail = u[tm - HALO:, :]
        tail_ref[...] = tail
        prev_ref[j] = tail
        return (u_ref[pl.ds(HALO - 2, tm), :] * cw_ref[0:1, :] + u_ref[pl.ds(HALO - 1, tm), :] * cw_ref[1:2, :]
                + u * cw_ref[2:3, :] + cb_ref[...])

    cg = conv_half(wg_ref, cwg_ref, cbg_ref, ug_ref, tg_ref, pg_ref)
    cv = conv_half(wv_ref, cwv_ref, cbv_ref, uv_ref, tv_ref, pv_ref)
    act = (_gelu_tanh(cg) * cv).astype(BF16)
    acc_ref[...] += _dot(act, wd_ref[...])

    @pl.when(j == pl.num_programs(1) - 1)
    def _():
        y_ref[...] = h_ref[...] + _rms(acc_ref[...], nw_ref[...])


def _ffn_prompt_call(a2, h, w_up, conv_w, conv_b, w_down, n_post, tm, tf):
    m, d = h.shape
    dff = w_down.shape[0]
    nf = dff // tf
    cb = conv_b.reshape(1, 2 * dff)
    return pl.pallas_call(
        _ffn_prompt_kernel,
        grid=(m // tm, nf),
        in_specs=[pl.BlockSpec((tm, d), lambda i, j: (i, 0)),
                  pl.BlockSpec((tm, d), lambda i, j: (i, 0)),
                  pl.BlockSpec((d, tf), lambda i, j: (0, j)),
                  pl.BlockSpec((d, tf), lambda i, j: (0, nf + j)),
                  pl.BlockSpec((conv_w.shape[0], tf), lambda i, j: (0, j)),
                  pl.BlockSpec((conv_w.shape[0], tf), lambda i, j: (0, nf + j)),
                  pl.BlockSpec((1, tf), lambda i, j: (0, j)),
                  pl.BlockSpec((1, tf), lambda i, j: (0, nf + j)),
                  pl.BlockSpec((tf, d), lambda i, j: (j, 0)),
                  pl.BlockSpec((1, d), lambda i, j: (0, 0))],
        out_specs=[pl.BlockSpec((tm, d), lambda i, j: (i, 0)),
                   pl.BlockSpec((HALO, tf), lambda i, j: (i, j)),
                   pl.BlockSpec((HALO, tf), lambda i, j: (i, j))],
        out_shape=[jax.ShapeDtypeStruct((m, d), F32),
                   jax.ShapeDtypeStruct((m // tm * HALO, dff), F32),
                   jax.ShapeDtypeStruct((m // tm * HALO, dff), F32)],
        scratch_shapes=[pltpu.VMEM((tm, d), F32), pltpu.VMEM((tm + HALO, tf), F32),
                        pltpu.VMEM((tm + HALO, tf), F32), pltpu.VMEM((nf, HALO, tf), F32),
                        pltpu.VMEM((nf, HALO, tf), F32)],
        compiler_params=_cparams(("arbitrary", "arbitrary")),
        name="ffn_prompt",
    )(a2, h, w_up, w_up, conv_w, conv_w, cb, cb, w_down, n_post)


def _ffn_sample_kernel(a_ref, h_ref, h0g_ref, h1g_ref, h0v_ref, h1v_ref, wg_ref, wv_ref, cwg_ref, cwv_ref,
                       cbg_ref, cbv_ref, wd_ref, nw_ref, y_ref, ug_ref, uv_ref, acc_ref):
    j = pl.program_id(0)

    @pl.when(j == 0)
    def _():
        acc_ref[...] = jnp.zeros(acc_ref.shape, F32)

    a = a_ref[...]

    def conv_half(w_ref, cw_ref, cb_ref, h0_ref, h1_ref, u_ref):
        u = _dot(a, w_ref[...])
        u_ref[...] = u
        return (h0_ref[...] * cw_ref[0:1, :] + h1_ref[...] * cw_ref[1:2, :] + u * cw_ref[2:3, :]
                + cb_ref[...])

    cg = conv_half(wg_ref, cwg_ref, cbg_ref, h0g_ref, h1g_ref, ug_ref)
    cv = conv_half(wv_ref, cwv_ref, cbv_ref, h0v_ref, h1v_ref, uv_ref)
    act = (_gelu_tanh(cg) * cv).astype(BF16)
    acc_ref[...] += _dot(act, wd_ref[...])

    @pl.when(j == pl.num_programs(0) - 1)
    def _():
        y_ref[...] = h_ref[...] + _rms(acc_ref[...], nw_ref[...])


def _ffn_sample_call(a2, h, hist, w_up, conv_w, conv_b, w_down, n_post, tf):
    b, d = h.shape
    dff = w_down.shape[0]
    nf = dff // tf
    cb = conv_b.reshape(1, 2 * dff)
    hist2 = hist.reshape(b, 2 * 2 * dff)
    nb = 2 * nf
    full = lambda w: pl.BlockSpec((b, w), lambda j: (0, 0))
    col = lambda base: pl.BlockSpec((b, tf), lambda j: (0, base + j))
    return pl.pallas_call(
        _ffn_sample_kernel,
        grid=(nf,),
        in_specs=[full(d), full(d), col(0), col(nb), col(nf), col(nb + nf),
                  pl.BlockSpec((d, tf), lambda j: (0, j)),
                  pl.BlockSpec((d, tf), lambda j: (0, nf + j)),
                  pl.BlockSpec((conv_w.shape[0], tf), lambda j: (0, j)),
                  pl.BlockSpec((conv_w.shape[0], tf), lambda j: (0, nf + j)),
                  pl.BlockSpec((1, tf), lambda j: (0, j)),
                  pl.BlockSpec((1, tf), lambda j: (0, nf + j)),
                  pl.BlockSpec((tf, d), lambda j: (j, 0)),
                  pl.BlockSpec((1, d), lambda j: (0, 0))],
        out_specs=[full(d), col(0), col(0)],
        out_shape=[jax.ShapeDtypeStruct((b, d), F32), jax.ShapeDtypeStruct((b, dff), F32),
                   jax.ShapeDtypeStruct((b, dff), F32)],
        scratch_shapes=[pltpu.VMEM((b, d), F32)],
        compiler_params=_cparams(("arbitrary",)),
        name="ffn_sample",
    )(a2, h, hist2, hist2, hist2, hist2, w_up, w_up, conv_w, conv_w, cb, cb, w_down, n_post)


def _rope_tables(pos, half):
    inv = ROPE_BASE ** (-2.0 * jnp.arange(half, dtype=F32) / (2 * half))
    ang = pos[:, None] * inv[None, :]
    zeros = jnp.zeros((pos.shape[0], LANES - 2 * half), F32)
    cos, sin = jnp.cos(ang), jnp.sin(ang)
    return (jnp.concatenate([cos, cos, zeros], axis=1), jnp.concatenate([sin, sin, zeros], axis=1))


def _tile(n, pref):
    return pref if n % pref == 0 else n


def kernel(x_prompt, x_sample, cache_kv_latent, cache_k_rope, state_gla, state_ffn_conv, page_table,
           norm_attn_pre, w_in, q_norm, w_uq, kv_norm, w_uk, w_uv, w_alpha, b_alpha, gla_norm, w_o,
           norm_attn_post, norm_ffn_pre, w_up, conv_w, conv_b, w_down, norm_ffn_post):
    depth = w_in.shape[0]
    _, seq, d = x_prompt.shape
    bs, t_new, _ = x_sample.shape
    assert x_prompt.shape[0] == 1 and t_new == 1
    q_rank = w_uq.shape[1]
    kv_rank, heads, nope = w_uk.shape[1:]
    v_head = w_uv.shape[3]
    rope = cache_k_rope.shape[3]
    half = rope // 2
    assert w_uq.shape[2] == heads * (nope + rope) and nope == LANES and v_head == LANES and rope <= LANES // 2
    _, _, gla_heads, dk, dv = state_gla.shape
    gla_qk, mix_gla = gla_heads * dk, gla_heads * dv
    gate_rank = w_alpha.shape[1]
    assert rope + gate_rank <= LANES
    dff = w_down.shape[1]
    assert conv_w.shape[1] == 3
    past = page_table.shape[1] * cache_kv_latent.shape[2]
    mla_scale = float((nope + rope) ** -0.5)
    dims = (q_rank, kv_rank, heads, nope, gla_qk, mix_gla, mla_scale, float(dk ** -0.5))

    ct_p, st_p = _rope_tables(jnp.arange(seq, dtype=F32), half)
    ct_s, st_s = _rope_tables(jnp.full((bs,), past, F32), half)
    cache_rope_t = jnp.swapaxes(cache_k_rope, 2, 3)
    n_pages = page_table.shape[1]
    pages_per_step = next(p for p in (16, 8, 4, 2) if n_pages % (p * DECODE_SLOTS) == 0)

    yp, ys = x_prompt[0], x_sample[:, 0]
    outs = {k: [] for k in ("lat_p", "rope_p", "gla_p", "conv_p", "lat_s", "rope_s", "gla_s", "conv_s")}
    for l in range(depth):
        sizes = (q_rank, kv_rank, rope, gla_qk, gla_qk, mix_gla, gate_rank, mix_gla)
        idx = np.cumsum(sizes)[:-1].tolist()
        w_cq, w_ckv, w_kr, w_gq, w_gk, w_gv, w_ga, w_gg = jnp.split(w_in[l], idx, axis=1)
        zc = lambda n: jnp.zeros((d, n), F32)
        lb = jnp.concatenate([w_kr, w_ga, zc(LANES - rope - gate_rank)], axis=1)
        lbs = jnp.concatenate([-w_kr[:, half:], w_kr[:, :half], zc(LANES - rope)], axis=1)
        win = jnp.concatenate([w_cq, w_ckv, w_gq, w_gk, w_gv, w_gg, lb, lbs], axis=1).astype(BF16)
        wq3 = w_uq[l].reshape(q_rank, heads, nope + rope)
        wq_n = wq3[:, :, :nope].reshape(q_rank, heads * nope)
        x1, x2 = wq3[:, :, nope:nope + half], wq3[:, :, nope + half:]
        zq = jnp.zeros((q_rank, heads, LANES - rope), F32)
        wq_r = jnp.concatenate([x1, x2, zq], axis=2).reshape(q_rank, heads * LANES)
        wq_rs = jnp.concatenate([-x2, x1, zq], axis=2).reshape(q_rank, heads * LANES)
        wq = jnp.concatenate([wq_n, wq_r, wq_rs], axis=1).astype(BF16)
        w_uk2 = w_uk[l].reshape(kv_rank, heads * nope).astype(BF16)
        w_uv2 = w_uv[l].reshape(kv_rank, heads * v_head).astype(BF16)
        w_ukT = jnp.transpose(w_uk[l], (1, 2, 0)).astype(BF16)
        w_uv_h = jnp.transpose(w_uv[l], (1, 0, 2)).astype(BF16)
        walpha = jnp.zeros((LANES, gla_qk), F32).at[rope:rope + gate_rank].set(w_alpha[l]).astype(BF16)
        row = lambda v: v.reshape(1, -1)
        common = (row(norm_attn_pre[l]), win, row(q_norm[l]), row(kv_norm[l]), wq)
        tail = (walpha, row(b_alpha[l]))
        wo_b = w_o[l].astype(BF16)
        w_up_b = w_up[l].astype(BF16)
        w_down_b = w_down[l].astype(BF16)
        gnw = row(gla_norm[l])

        tm = _tile(seq, 256)
        (q, k, v, ckv_p, kr_p, gq, gk, gv, g, gg) = _proj_call(
            yp, ct_p, st_p, common + (w_uk2, w_uv2) + tail, dims, rope, False, tm)
        o_mla = _flash_call(q, k, v, heads, nope + LANES, v_head, _tile(seq, 1024))
        o_gla, gla_new_p = _gla_chunk_call(gq, gk, gv, g, gg, gnw, gla_heads, dk, dv, _tile(seq, 64))
        h_p, a2_p = _post_attn_call(yp, o_mla, o_gla, wo_b, row(norm_attn_post[l]), row(norm_ffn_pre[l]),
                                    _tile(seq, 512))
        yp, tail_g, tail_v = _ffn_prompt_call(a2_p, h_p, w_up_b, conv_w[l], conv_b[l], w_down_b,
                                              row(norm_ffn_post[l]), _tile(seq, 512), _tile(dff, 512))
        conv_new_p = jnp.concatenate([tail_g[-2:], tail_v[-2:]], axis=1)[None]

        (q_abs, q_rope, ckv_s, kr_s, sq, sk, sv, sg, sgg) = _proj_call(
            ys, ct_s, st_s, common + (w_ukT, w_uv2) + tail, dims, rope, True, bs)
        q_abs3 = q_abs.reshape(bs, heads, kv_rank)
        q_t = jnp.pad(jnp.swapaxes(q_abs3, 1, 2), ((0, 0), (0, 0), (0, LANES - heads)))
        lat_out = _decode_call(page_table, q_t, q_abs3, q_rope.reshape(bs, heads, LANES),
                               ckv_s.reshape(bs, 1, kv_rank), kr_s.reshape(bs, 1, rope),
                               cache_kv_latent, cache_rope_t, l, pages_per_step)
        o_mla_s = _uv_call(lat_out.reshape(bs, heads * kv_rank), w_uv_h)
        sh = lambda z, w: z.reshape(bs, gla_heads, w)
        o_gla_s, gla_new_s = _gla_step_call(sh(sq, dk), sh(sk, dk), sh(sg, dk), sh(sv, dv), sh(sgg, dv), gnw,
                                            state_gla, l)
        h_s, a2_s = _post_attn_call(ys, o_mla_s, o_gla_s.reshape(bs, mix_gla), wo_b, row(norm_attn_post[l]),
                                    row(norm_ffn_pre[l]), bs)
        ys, u_g, u_v = _ffn_sample_call(a2_s, h_s, state_ffn_conv[l], w_up_b, conv_w[l], conv_b[l], w_down_b,
                                        row(norm_ffn_post[l]), _tile(dff, 512))
        conv_new_s = jnp.stack([state_ffn_conv[l][:, 1], jnp.concatenate([u_g, u_v], axis=1)], axis=1)

        outs["lat_p"].append(ckv_p[None]); outs["rope_p"].append(kr_p[None])
        outs["gla_p"].append(gla_new_p[None].astype(x_prompt.dtype)); outs["conv_p"].append(conv_new_p)
        outs["lat_s"].append(ckv_s[:, None]); outs["rope_s"].append(kr_s[:, None])
        outs["gla_s"].append(gla_new_s.astype(state_gla.dtype)); outs["conv_s"].append(conv_new_s)

    st = lambda name: jnp.stack(outs[name])
    return (yp[None], ys[:, None], st("lat_p"), st("rope_p"), st("gla_p"), st("conv_p"),
            st("lat_s"), st("rope_s"), st("gla_s"), st("conv_s"))
```

```python
import functools

import numpy as np
import jax
import jax.numpy as jnp
from jax import lax
from jax.experimental import pallas as pl
from jax.experimental.pallas import tpu as pltpu

EPS = 1e-6
ROPE_BASE = 10000.0
GATE_NORM = 16.0
LANES = 128
V7X_VMEM_LIMIT_BYTES = 56 * 1024 * 1024

F32 = jnp.float32
BF16 = jnp.bfloat16
NEG_INF = float("-inf")


def _cparams(sem):
    return pltpu.CompilerParams(dimension_semantics=sem, vmem_limit_bytes=V7X_VMEM_LIMIT_BYTES)


def _rms(x, w):
    return x * lax.rsqrt(jnp.mean(x * x, axis=-1, keepdims=True) + EPS) * w


def _dot(a, b):
    return jnp.dot(a, b, preferred_element_type=F32)


def _dot_nt(a, b):
    return lax.dot_general(a, b, (((1,), (1,)), ((), ())), preferred_element_type=F32)


def _dot_tn(a, b):
    return lax.dot_general(a, b, (((0,), (0,)), ((), ())), preferred_element_type=F32)


def _const_spec(shape):
    nd = len(shape)
    return pl.BlockSpec(shape, lambda *_: (0,) * nd, pipeline_mode=pl.Buffered(1))


def _proj_kernel(dims, sample, x_ref, ct_ref, st_ref, wn_ref, win_ref, qn_ref, kvn_ref, wq_ref,
                 wa_ref, wb_ref, walpha_ref, balpha_ref, *outs):
    (q_rank, kv_rank, heads, nope, gla_qk, mix_gla, mla_scale, gla_qscale) = dims
    a = _rms(x_ref[...], wn_ref[...]).astype(BF16)

    off = [0]

    def mm(width):
        lo = off[0]
        off[0] += width
        return _dot(a, win_ref[:, lo:lo + width])

    cq = mm(q_rank)
    ckv = mm(kv_rank)
    gq = mm(gla_qk)
    gk = mm(gla_qk)
    gv = mm(mix_gla)
    gg = mm(mix_gla)
    lb = mm(LANES)
    lbs = mm(LANES)

    ct = ct_ref[...]
    st = st_ref[...]

    cqn = _rms(cq, qn_ref[...]).astype(BF16)
    hn = heads * nope
    q_nope = _dot(cqn, wq_ref[:, 0:hn]) * mla_scale
    q_r = _dot(cqn, wq_ref[:, hn:hn + heads * LANES])
    q_rs = _dot(cqn, wq_ref[:, hn + heads * LANES:hn + 2 * heads * LANES])

    c_kv = _rms(ckv, kvn_ref[...])
    c_kv_b = c_kv.astype(BF16)
    k_rope_blk = lb * ct + lbs * st

    glog = _dot(lb.astype(BF16), walpha_ref[...]) + balpha_ref[...]
    g = (jnp.minimum(glog, 0.0) - jnp.log1p(jnp.exp(-jnp.abs(glog)))) * (1.0 / GATE_NORM)

    if sample:
        qabs_ref, qrope_ref, ckv_ref, kr_ref, gq_ref, gk_ref, gv_ref, g_ref, gg_ref = outs
        for h in range(heads):
            qn_h = q_nope[:, h * nope:(h + 1) * nope].astype(BF16)
            qabs_ref[:, h * kv_rank:(h + 1) * kv_rank] = _dot(qn_h, wa_ref[h]).astype(BF16)
            blk = slice(h * LANES, (h + 1) * LANES)
            qrope_ref[:, blk] = ((q_r[:, blk] * ct + q_rs[:, blk] * st) * mla_scale).astype(BF16)
    else:
        q_ref, k_ref, v_ref, ckv_ref, kr_ref, gq_ref, gk_ref, gv_ref, g_ref, gg_ref = outs
        k_nope = _dot(c_kv_b, wa_ref[...])
        v_ref[...] = _dot(c_kv_b, wb_ref[...]).astype(BF16)
        k_rope_b = k_rope_blk.astype(BF16)
        for h in range(heads):
            blk = slice(h * LANES, (h + 1) * LANES)
            base = h * (nope + LANES)
            q_ref[:, base:base + nope] = q_nope[:, h * nope:(h + 1) * nope].astype(BF16)
            q_ref[:, base + nope:base + nope + LANES] = (
                (q_r[:, blk] * ct + q_rs[:, blk] * st) * mla_scale).astype(BF16)
            k_ref[:, base:base + nope] = k_nope[:, h * nope:(h + 1) * nope].astype(BF16)
            k_ref[:, base + nope:base + nope + LANES] = k_rope_b

    ckv_ref[...] = c_kv
    kr_ref[...] = k_rope_blk[:, 0:kr_ref.shape[1]]
    gq_ref[...] = gq * gla_qscale
    gk_ref[...] = gk
    gv_ref[...] = gv
    g_ref[...] = g
    gg_ref[...] = gg


def _proj_call(x, ct, st, wts, dims, rope, sample, tm):
    (q_rank, kv_rank, heads, nope, gla_qk, mix_gla, _, _) = dims
    m, d = x.shape
    row = lambda w: pl.BlockSpec((tm, w), lambda i: (i, 0))
    in_specs = [row(d), row(LANES), row(LANES)] + [_const_spec(w.shape) for w in wts]
    gla_shapes = [(m, gla_qk), (m, gla_qk), (m, mix_gla), (m, gla_qk), (m, mix_gla)]
    if sample:
        out_shapes = [((m, heads * kv_rank), BF16), ((m, heads * LANES), BF16)]
    else:
        out_shapes = [((m, heads * (nope + LANES)), BF16), ((m, heads * (nope + LANES)), BF16),
                      ((m, heads * nope), BF16)]
    out_shapes += [((m, kv_rank), F32), ((m, rope), F32)] + [(s, F32) for s in gla_shapes]
    return pl.pallas_call(
        functools.partial(_proj_kernel, dims, sample),
        grid=(m // tm,),
        in_specs=in_specs,
        out_specs=[row(s[1]) for s, _ in out_shapes],
        out_shape=[jax.ShapeDtypeStruct(s, dt) for s, dt in out_shapes],
        compiler_params=_cparams(("parallel",)),
        name="proj_sample" if sample else "proj_prompt",
    )(x, ct, st, *wts)


FLASH_SUB = 512


def _flash_kernel(qi_ref, kj_ref, q_ref, k_ref, v_ref, o_ref, m_ref, l_ref, acc_ref):
    t = pl.program_id(1)
    i = qi_ref[t]
    j = kj_ref[t]
    tq, tk = q_ref.shape[0], k_ref.shape[0]

    @pl.when(j == 0)
    def _():
        m_ref[...] = jnp.full(m_ref.shape, NEG_INF, F32)
        l_ref[...] = jnp.zeros(l_ref.shape, F32)
        acc_ref[...] = jnp.zeros(acc_ref.shape, F32)

    def update(masked):
        ts = FLASH_SUB if tk % FLASH_SUB == 0 else tk
        q = q_ref[...]
        m, l, acc = m_ref[...], l_ref[...], acc_ref[...]
        for c0 in range(0, tk, ts):
            s = _dot_nt(q, k_ref[c0:c0 + ts, :])
            if masked:
                rowi = lax.broadcasted_iota(jnp.int32, (tq, ts), 0)
                coli = lax.broadcasted_iota(jnp.int32, (tq, ts), 1) + c0
                s = jnp.where(coli <= rowi, s, NEG_INF)
            m_new = jnp.maximum(m, jnp.max(s, axis=-1, keepdims=True))
            alpha = jnp.exp(m - m_new)
            p = jnp.exp(s - jnp.tile(m_new, (1, ts // LANES)))
            l = alpha * l + jnp.sum(p, axis=-1, keepdims=True)
            acc = alpha * acc + _dot(p.astype(BF16), v_ref[c0:c0 + ts, :])
            m = m_new
        m_ref[...], l_ref[...], acc_ref[...] = m, l, acc

    @pl.when(j < i)
    def _():
        update(False)

    @pl.when(j == i)
    def _():
        update(True)
        o_ref[...] = (acc_ref[...] / l_ref[...]).astype(o_ref.dtype)


def _flash_call(q, k, v, heads, qk_w, v_w, t):
    assert v_w == LANES
    s = q.shape[0]
    n = s // t
    pairs = [(i, j) for i in range(n) for j in range(i + 1)]
    qi = jnp.asarray([p[0] for p in pairs], jnp.int32)
    kj = jnp.asarray([p[1] for p in pairs], jnp.int32)
    grid_spec = pltpu.PrefetchScalarGridSpec(
        num_scalar_prefetch=2,
        grid=(heads, len(pairs)),
        in_specs=[pl.BlockSpec((t, qk_w), lambda h, p, qi, kj: (qi[p], h)),
                  pl.BlockSpec((t, qk_w), lambda h, p, qi, kj: (kj[p], h)),
                  pl.BlockSpec((t, v_w), lambda h, p, qi, kj: (kj[p], h))],
        out_specs=pl.BlockSpec((t, v_w), lambda h, p, qi, kj: (qi[p], h)),
        scratch_shapes=[pltpu.VMEM((t, LANES), F32), pltpu.VMEM((t, LANES), F32),
                        pltpu.VMEM((t, v_w), F32)],
    )
    return pl.pallas_call(
        _flash_kernel,
        grid_spec=grid_spec,
        out_shape=jax.ShapeDtypeStruct((s, heads * v_w), BF16),
        compiler_params=_cparams(("parallel", "arbitrary")),
        name="mla_prompt_flash",
    )(qi, kj, q, k, v)


def _gla_chunk_kernel(heads, dk, dv, q_ref, k_ref, v_ref, g_ref, gg_ref, nw_ref, o_ref, sfin_ref,
                      state_ref):
    c = pl.program_id(0)
    ch = q_ref.shape[0]

    @pl.when(c == 0)
    def _():
        state_ref[...] = jnp.zeros(state_ref.shape, F32)

    rowi = lax.broadcasted_iota(jnp.int32, (ch, ch), 0)
    coli = lax.broadcasted_iota(jnp.int32, (ch, ch), 1)
    causal = coli <= rowi
    tri = causal.astype(F32)
    nw = nw_ref[...]
    for h in range(heads):
        ks = slice(h * dk, (h + 1) * dk)
        vs = slice(h * dv, (h + 1) * dv)
        b = jnp.dot(tri, g_ref[:, ks], preferred_element_type=F32, precision=lax.Precision.HIGHEST)
        b_last = b[ch - 1:ch, :]
        b_mid = b[ch // 2:ch // 2 + 1, :]
        q = q_ref[:, ks]
        qb = (q * jnp.exp(b)).astype(BF16)
        qm = (q * jnp.exp(b - b_mid)).astype(BF16)
        km = (k_ref[:, ks] * jnp.exp(b_mid - b)).astype(BF16)
        kl = (k_ref[:, ks] * jnp.exp(b_last - b)).astype(BF16)
        vb = v_ref[:, vs].astype(BF16)
        state = state_ref[h]
        attn = jnp.where(causal, _dot_nt(qm, km), 0.0)
        o = _dot(qb, state.astype(BF16)) + _dot(attn.astype(BF16), vb)
        decay_col = jnp.transpose(jnp.broadcast_to(jnp.exp(b_last), (dk, dk)))[:, 0:1]
        state_ref[h] = decay_col * state + _dot_tn(kl, vb)
        gate = gg_ref[:, vs]
        o_ref[:, vs] = (_rms(o, nw) * (gate * jax.nn.sigmoid(gate))).astype(o_ref.dtype)

    @pl.when(c == pl.num_programs(0) - 1)
    def _():
        sfin_ref[...] = state_ref[...]


def _gla_chunk_call(gq, gk, gv, g, gg, nw, heads, dk, dv, ch):
    s = gq.shape[0]
    row = lambda w: pl.BlockSpec((ch, w), lambda c: (c, 0))
    return pl.pallas_call(
        functools.partial(_gla_chunk_kernel, heads, dk, dv),
        grid=(s // ch,),
        in_specs=[row(heads * dk), row(heads * dk), row(heads * dv), row(heads * dk), row(heads * dv),
                  pl.BlockSpec((1, dv), lambda c: (0, 0))],
        out_specs=[row(heads * dv), pl.BlockSpec((heads, dk, dv), lambda c: (0, 0, 0))],
        out_shape=[jax.ShapeDtypeStruct((s, heads * dv), BF16),
                   jax.ShapeDtypeStruct((heads, dk, dv), F32)],
        scratch_shapes=[pltpu.VMEM((heads, dk, dv), F32)],
        compiler_params=_cparams(("arbitrary",)),
        name="gla_prompt_chunked",
    )(gq, gk, gv, g, gg, nw)


DECODE_SLOTS = 8
DECODE_AHEAD = 6


def _decode_kernel(pages, n_chunks, layer, pt_ref, qa_ref, qr_ref, cn_ref, kn_ref, lat_hbm, rk_hbm,
                   o_ref, lat_buf, rk_buf, sem):
    b = pl.program_id(0)
    total = pl.num_programs(0) * n_chunks
    heads, kv_rank = o_ref.shape
    pg = lat_hbm.shape[2]
    rope_w = rk_hbm.shape[2]
    reps = kv_rank // LANES

    def chunk_copies(bi, ci, slot):
        cps = []
        for p in range(pages):
            phys = pt_ref[bi, ci * pages + p]
            cps.append(pltpu.make_async_copy(lat_hbm.at[layer, phys], lat_buf.at[slot, pl.ds(p * pg, pg)],
                                             sem.at[0, slot]))
            cps.append(pltpu.make_async_copy(rk_hbm.at[layer, phys], rk_buf.at[slot, p], sem.at[1, slot]))
        return cps

    def start_chunk(bi, ci):
        for n, cp in enumerate(chunk_copies(bi, ci, ci % DECODE_SLOTS)):
            cp.start(priority=(n // 2) % 2)

    @pl.when(b == 0)
    def _():
        for c in range(DECODE_AHEAD):
            start_chunk(0, c)

    qa = qa_ref[...].astype(F32)
    qt = jnp.transpose(jnp.concatenate([qa, jnp.zeros((LANES - heads, kv_rank), F32)], axis=0))
    qr = qr_ref[:, 0:rope_w].astype(F32)
    half = pages * pg // 2

    def latent_scores(c):
        slot = c % DECODE_SLOTS
        for cp in chunk_copies(b, c, slot):
            cp.wait()
        return (_dot(lat_buf[slot, 0:half], qt), _dot(lat_buf[slot, half:], qt))

    m = jnp.full((heads, LANES), NEG_INF, F32)
    l = jnp.zeros((heads, LANES), F32)
    acc = jnp.zeros((heads, kv_rank), F32)
    st = latent_scores(0)
    for c in range(n_chunks):
        slot = c % DECODE_SLOTS
        ahead = c + DECODE_AHEAD
        nb, nc = (b, ahead) if ahead < n_chunks else (b + 1, ahead - n_chunks)

        @pl.when(b * n_chunks + ahead < total)
        def _():
            start_chunk(nb, nc)

        st_next = latent_scores(c + 1) if c + 1 < n_chunks else None
        per_half = pages // 2
        s = jnp.concatenate(
            [jnp.transpose(st[p // per_half][(p % per_half) * pg:(p % per_half + 1) * pg, :])[0:heads, :]
             + _dot(qr, rk_buf[slot, p]) for p in range(pages)], axis=1)
        m_new = jnp.maximum(m, jnp.max(s, axis=-1, keepdims=True))
        alpha = jnp.exp(m - m_new)
        p_un = jnp.exp(s - jnp.tile(m_new, (1, pages * pg // LANES)))
        l = alpha * l + jnp.sum(p_un, axis=-1, keepdims=True)
        acc = jnp.tile(alpha, (1, reps)) * acc + _dot(p_un, lat_buf[slot])
        m = m_new
        st = st_next

    cn = cn_ref[...]
    kn = kn_ref[...]
    s_self = (jnp.sum(qa * cn, axis=-1, keepdims=True)
              + jnp.sum(qr * kn, axis=-1, keepdims=True))
    m_new = jnp.maximum(m, s_self)
    alpha = jnp.exp(m - m_new)
    p_self = jnp.exp(s_self - m_new)
    l_fin = alpha * l + p_self
    acc_fin = jnp.tile(alpha, (1, reps)) * acc + jnp.tile(p_self, (1, reps)) * cn
    o_ref[...] = (acc_fin / jnp.tile(l_fin, (1, reps))).astype(o_ref.dtype)


def _decode_call(page_table, q_abs, q_rope, ckv_new, kr_new, cache_lat, cache_rope_t, layer, pages):
    b, heads, kv_rank = q_abs.shape
    n_pages = page_table.shape[1]
    page = cache_lat.shape[2]
    rope_w = cache_rope_t.shape[2]
    n_chunks = n_pages // pages
    assert n_chunks % DECODE_SLOTS == 0 and DECODE_AHEAD < DECODE_SLOTS - 1 and pages % 2 == 0

    per_b = lambda *shape: pl.BlockSpec((None,) + shape, lambda bi, pt: (bi,) + (0,) * len(shape))
    grid_spec = pltpu.PrefetchScalarGridSpec(
        num_scalar_prefetch=1,
        grid=(b,),
        in_specs=[per_b(heads, kv_rank), per_b(heads, LANES), per_b(1, kv_rank),
                  per_b(1, rope_w), pl.BlockSpec(memory_space=pl.ANY), pl.BlockSpec(memory_space=pl.ANY)],
        out_specs=per_b(heads, kv_rank),
        scratch_shapes=[pltpu.VMEM((DECODE_SLOTS, pages * page, kv_rank), F32),
                        pltpu.VMEM((DECODE_SLOTS, pages, rope_w, page), F32),
                        pltpu.SemaphoreType.DMA((2, DECODE_SLOTS))],
    )
    return pl.pallas_call(
        functools.partial(_decode_kernel, pages, n_chunks, layer),
        grid_spec=grid_spec,
        out_shape=jax.ShapeDtypeStruct((b, heads, kv_rank), BF16),
        compiler_params=_cparams(("arbitrary",)),
        name="mla_paged_decode",
    )(page_table, q_abs, q_rope, ckv_new, kr_new, cache_lat, cache_rope_t)


def _uv_kernel(x_ref, w_ref, o_ref):
    o_ref[...] = _dot(x_ref[...], w_ref[...]).astype(o_ref.dtype)


def _uv_call(lat_out, w_uv_h):
    heads, c, dv = w_uv_h.shape
    b = lat_out.shape[0]
    return pl.pallas_call(
        _uv_kernel,
        grid=(heads,),
        in_specs=[pl.BlockSpec((b, c), lambda h: (0, h)),
                  pl.BlockSpec((None, c, dv), lambda h: (h, 0, 0))],
        out_specs=pl.BlockSpec((b, dv), lambda h: (0, h)),
        out_shape=jax.ShapeDtypeStruct((b, heads * dv), BF16),
        compiler_params=_cparams(("parallel",)),
        name="mla_decode_value_up",
    )(lat_out, w_uv_h)


def _gla_step_kernel(heads, q_ref, k_ref, g_ref, v_ref, gg_ref, nw_ref, s_ref, o_ref, snew_ref):
    bb, _, dk = q_ref.shape
    rows = [r[i] for i in range(bb) for r in (q_ref, k_ref)] + [jnp.exp(g_ref[i]) for i in range(bb)]
    pad = jnp.zeros((dk - 3 * heads * bb, dk), F32)
    cols = jnp.transpose(jnp.concatenate(rows + [pad], axis=0))
    nw = nw_ref[...]
    for i in range(bb):
        for h in range(heads):
            cq, ck, ce = (2 * i) * heads + h, (2 * i + 1) * heads + h, (2 * bb + i) * heads + h
            s_new = cols[:, ce:ce + 1] * s_ref[i, h] + cols[:, ck:ck + 1] * v_ref[i, h:h + 1, :]
            snew_ref[i, h] = s_new
            o = jnp.sum(cols[:, cq:cq + 1] * s_new, axis=0, keepdims=True)
            gate = gg_ref[i, h:h + 1, :]
            o_ref[i, h:h + 1, :] = (_rms(o, nw) * (gate * jax.nn.sigmoid(gate))).astype(o_ref.dtype)


def _gla_step_call(gq, gk, g, gv, gg, nw, state, layer):
    b, heads, dk = gq.shape
    dv = gv.shape[2]
    bb = next(n for n in (8, 4, 2, 1) if b % n == 0 and 3 * heads * n <= dk)
    per_b = lambda w: pl.BlockSpec((bb, heads, w), lambda bi: (bi, 0, 0))
    return pl.pallas_call(
        functools.partial(_gla_step_kernel, heads),
        grid=(b // bb,),
        in_specs=[per_b(dk), per_b(dk), per_b(dk), per_b(dv), per_b(dv),
                  pl.BlockSpec((1, dv), lambda bi: (0, 0)),
                  pl.BlockSpec((None, bb, heads, dk, dv), lambda bi: (layer, bi, 0, 0, 0))],
        out_specs=[per_b(dv), pl.BlockSpec((bb, heads, dk, dv), lambda bi: (bi, 0, 0, 0))],
        out_shape=[jax.ShapeDtypeStruct((b, heads, dv), BF16),
                   jax.ShapeDtypeStruct((b, heads, dk, dv), F32)],
        compiler_params=_cparams(("parallel",)),
        name="gla_sample_step",
    )(gq, gk, g, gv, gg, nw, state)


def _post_attn_kernel(x_ref, om_ref, og_ref, wo_ref, np_ref, nf_ref, h_ref, a_ref):
    nm = om_ref.shape[1]
    mix = _dot(om_ref[...], wo_ref[0:nm, :]) + _dot(og_ref[...], wo_ref[nm:, :])
    h = x_ref[...] + _rms(mix, np_ref[...])
    h_ref[...] = h
    a_ref[...] = _rms(h, nf_ref[...]).astype(a_ref.dtype)


def _post_attn_call(x, o_mla, o_gla, wo, n_post, n_ffn, tm):
    m, d = x.shape
    row = lambda w: pl.BlockSpec((tm, w), lambda i: (i, 0))
    return pl.pallas_call(
        _post_attn_kernel,
        grid=(m // tm,),
        in_specs=[row(d), row(o_mla.shape[1]), row(o_gla.shape[1]), _const_spec(wo.shape),
                  _const_spec(n_post.shape), _const_spec(n_ffn.shape)],
        out_specs=[row(d), row(d)],
        out_shape=[jax.ShapeDtypeStruct((m, d), F32), jax.ShapeDtypeStruct((m, d), BF16)],
        compiler_params=_cparams(("parallel",)),
        name="attn_out_proj",
    )(x, o_mla, o_gla, wo, n_post, n_ffn)


HALO = 8

def _gelu_tanh(x):
    return 0.5 * x * (1.0 + jnp.tanh(np.sqrt(2.0 / np.pi) * (x + 0.044715 * (x * x * x))))


def _ffn_prompt_kernel(a_ref, h_ref, wg_ref, wv_ref, cwg_ref, cwv_ref, cbg_ref, cbv_ref, wd_ref,
                       nw_ref, y_ref, tg_ref, tv_ref, acc_ref, ug_ref, uv_ref, pg_ref, pv_ref):
    i = pl.program_id(0)
    j = pl.program_id(1)
    tm = a_ref.shape[0]

    @pl.when(j == 0)
    def _():
        acc_ref[...] = jnp.zeros(acc_ref.shape, F32)

    @pl.when(i == 0)
    def _():
        pg_ref[j] = jnp.zeros(pg_ref.shape[1:], F32)
        pv_ref[j] = jnp.zeros(pv_ref.shape[1:], F32)

    a = a_ref[...]

    def conv_half(w_ref, cw_ref, cb_ref, u_ref, tail_ref, prev_ref):
        u_ref[0:HALO, :] = prev_ref[j]
        u = _dot(a, w_ref[...])
        u_ref[HALO:, :] = u
        tail = u[tm - HALO:, :]
        tail_ref[...] = tail
        prev_ref[j] = tail
        return (u_ref[pl.ds(HALO - 2, tm), :] * cw_ref[0:1, :] + u_ref[pl.ds(HALO - 1, tm), :] * cw_ref[1:2, :]
                + u * cw_ref[2:3, :] + cb_ref[...])

    cg = conv_half(wg_ref, cwg_ref, cbg_ref, ug_ref, tg_ref, pg_ref)
    cv = conv_half(wv_ref, cwv_ref, cbv_ref, uv_ref, tv_ref, pv_ref)
    act = (_gelu_tanh(cg) * cv).astype(BF16)
    acc_ref[...] += _dot(act, wd_ref[...])

    @pl.when(j == pl.num_programs(1) - 1)
    def _():
        y_ref[...] = h_ref[...] + _rms(acc_ref[...], nw_ref[...])


def _ffn_prompt_call(a2, h, w_up, conv_w, conv_b, w_down, n_post, tm, tf):
    m, d = h.shape
    dff = w_down.shape[0]
    nf = dff // tf
    cb = conv_b.reshape(1, 2 * dff)
    return pl.pallas_call(
        _ffn_prompt_kernel,
        grid=(m // tm, nf),
        in_specs=[pl.BlockSpec((tm, d), lambda i, j: (i, 0)),
                  pl.BlockSpec((tm, d), lambda i, j: (i, 0)),
                  pl.BlockSpec((d, tf), lambda i, j: (0, j)),
                  pl.BlockSpec((d, tf), lambda i, j: (0, nf + j)),
                  pl.BlockSpec((conv_w.shape[0], tf), lambda i, j: (0, j)),
                  pl.BlockSpec((conv_w.shape[0], tf), lambda i, j: (0, nf + j)),
                  pl.BlockSpec((1, tf), lambda i, j: (0, j)),
                  pl.BlockSpec((1, tf), lambda i, j: (0, nf + j)),
                  pl.BlockSpec((tf, d), lambda i, j: (j, 0)),
                  pl.BlockSpec((1, d), lambda i, j: (0, 0))],
        out_specs=[pl.BlockSpec((tm, d), lambda i, j: (i, 0)),
                   pl.BlockSpec((HALO, tf), lambda i, j: (i, j)),
                   pl.BlockSpec((HALO, tf), lambda i, j: (i, j))],
        out_shape=[jax.ShapeDtypeStruct((m, d), F32),
                   jax.ShapeDtypeStruct((m // tm * HALO, dff), F32),
                   jax.ShapeDtypeStruct((m // tm * HALO, dff), F32)],
        scratch_shapes=[pltpu.VMEM((tm, d), F32), pltpu.VMEM((tm + HALO, tf), F32),
                        pltpu.VMEM((tm + HALO, tf), F32), pltpu.VMEM((nf, HALO, tf), F32),
                        pltpu.VMEM((nf, HALO, tf), F32)],
        compiler_params=_cparams(("arbitrary", "arbitrary")),
        name="ffn_prompt",
    )(a2, h, w_up, w_up, conv_w, conv_w, cb, cb, w_down, n_post)


def _ffn_sample_kernel(a_ref, h_ref, h0g_ref, h1g_ref, h0v_ref, h1v_ref, wg_ref, wv_ref, cwg_ref, cwv_ref,
                       cbg_ref, cbv_ref, wd_ref, nw_ref, y_ref, ug_ref, uv_ref, acc_ref):
    j = pl.program_id(0)

    @pl.when(j == 0)
    def _():
        acc_ref[...] = jnp.zeros(acc_ref.shape, F32)

    a = a_ref[...]

    def conv_half(w_ref, cw_ref, cb_ref, h0_ref, h1_ref, u_ref):
        u = _dot(a, w_ref[...])
        u_ref[...] = u
        return (h0_ref[...] * cw_ref[0:1, :] + h1_ref[...] * cw_ref[1:2, :] + u * cw_ref[2:3, :]
                + cb_ref[...])

    cg = conv_half(wg_ref, cwg_ref, cbg_ref, h0g_ref, h1g_ref, ug_ref)
    cv = conv_half(wv_ref, cwv_ref, cbv_ref, h0v_ref, h1v_ref, uv_ref)
    act = (_gelu_tanh(cg) * cv).astype(BF16)
    acc_ref[...] += _dot(act, wd_ref[...])

    @pl.when(j == pl.num_programs(0) - 1)
    def _():
        y_ref[...] = h_ref[...] + _rms(acc_ref[...], nw_ref[...])


def _ffn_sample_call(a2, h, hist, w_up, conv_w, conv_b, w_down, n_post, tf):
    b, d = h.shape
    dff = w_down.shape[0]
    nf = dff // tf
    cb = conv_b.reshape(1, 2 * dff)
    full = lambda w: pl.BlockSpec((b, w), lambda j: (0, 0))
    col = lambda base: pl.BlockSpec((b, tf), lambda j: (0, base + j))
    prev = lambda r, base: pl.BlockSpec((None, b, tf), lambda j: (r, 0, base + j))
    return pl.pallas_call(
        _ffn_sample_kernel,
        grid=(nf,),
        in_specs=[full(d), full(d), prev(0, 0), prev(1, 0), prev(0, nf), prev(1, nf),
                  pl.BlockSpec((d, tf), lambda j: (0, j)),
                  pl.BlockSpec((d, tf), lambda j: (0, nf + j)),
                  pl.BlockSpec((conv_w.shape[0], tf), lambda j: (0, j)),
                  pl.BlockSpec((conv_w.shape[0], tf), lambda j: (0, nf + j)),
                  pl.BlockSpec((1, tf), lambda j: (0, j)),
                  pl.BlockSpec((1, tf), lambda j: (0, nf + j)),
                  pl.BlockSpec((tf, d), lambda j: (j, 0)),
                  pl.BlockSpec((1, d), lambda j: (0, 0))],
        out_specs=[full(d), col(0), col(0)],
        out_shape=[jax.ShapeDtypeStruct((b, d), F32), jax.ShapeDtypeStruct((b, dff), F32),
                   jax.ShapeDtypeStruct((b, dff), F32)],
        scratch_shapes=[pltpu.VMEM((b, d), F32)],
        compiler_params=_cparams(("arbitrary",)),
        name="ffn_sample",
    )(a2, h, hist, hist, hist, hist, w_up, w_up, conv_w, conv_w, cb, cb, w_down, n_post)


def _rope_tables(pos, half):
    inv = ROPE_BASE ** (-2.0 * jnp.arange(half, dtype=F32) / (2 * half))
    ang = pos[:, None] * inv[None, :]
    zeros = jnp.zeros((pos.shape[0], LANES - 2 * half), F32)
    cos, sin = jnp.cos(ang), jnp.sin(ang)
    return (jnp.concatenate([cos, cos, zeros], axis=1), jnp.concatenate([sin, sin, zeros], axis=1))


def _tile(n, pref):
    return pref if n % pref == 0 else n


def kernel(x_prompt, x_sample, cache_kv_latent, cache_k_rope, state_gla, state_ffn_conv, page_table,
           norm_attn_pre, w_in, q_norm, w_uq, kv_norm, w_uk, w_uv, w_alpha, b_alpha, gla_norm, w_o,
           norm_attn_post, norm_ffn_pre, w_up, conv_w, conv_b, w_down, norm_ffn_post):
    depth = w_in.shape[0]
    _, seq, d = x_prompt.shape
    bs, t_new, _ = x_sample.shape
    assert x_prompt.shape[0] == 1 and t_new == 1
    q_rank = w_uq.shape[1]
    kv_rank, heads, nope = w_uk.shape[1:]
    v_head = w_uv.shape[3]
    rope = cache_k_rope.shape[3]
    half = rope // 2
    assert w_uq.shape[2] == heads * (nope + rope) and nope == LANES and v_head == LANES and rope <= LANES // 2
    _, _, gla_heads, dk, dv = state_gla.shape
    gla_qk, mix_gla = gla_heads * dk, gla_heads * dv
    gate_rank = w_alpha.shape[1]
    assert rope + gate_rank <= LANES
    dff = w_down.shape[1]
    assert conv_w.shape[1] == 3
    past = page_table.shape[1] * cache_kv_latent.shape[2]
    mla_scale = float((nope + rope) ** -0.5)
    dims = (q_rank, kv_rank, heads, nope, gla_qk, mix_gla, mla_scale, float(dk ** -0.5))

    ct_p, st_p = _rope_tables(jnp.arange(seq, dtype=F32), half)
    ct_s, st_s = _rope_tables(jnp.full((bs,), past, F32), half)
    cache_rope_t = jnp.swapaxes(cache_k_rope, 2, 3)
    n_pages = page_table.shape[1]
    pages_per_step = next(p for p in (16, 8, 4, 2) if n_pages % (p * DECODE_SLOTS) == 0)

    yp, ys = x_prompt[0], x_sample[:, 0]
    outs = {k: [] for k in ("lat_p", "rope_p", "gla_p", "conv_p", "lat_s", "rope_s", "gla_s", "conv_s")}
    for l in range(depth):
        sizes = (q_rank, kv_rank, rope, gla_qk, gla_qk, mix_gla, gate_rank, mix_gla)
        idx = np.cumsum(sizes)[:-1].tolist()
        w_cq, w_ckv, w_kr, w_gq, w_gk, w_gv, w_ga, w_gg = jnp.split(w_in[l], idx, axis=1)
        zc = lambda n: jnp.zeros((d, n), F32)
        lb = jnp.concatenate([w_kr, w_ga, zc(LANES - rope - gate_rank)], axis=1)
        lbs = jnp.concatenate([-w_kr[:, half:], w_kr[:, :half], zc(LANES - rope)], axis=1)
        win = jnp.concatenate([w_cq, w_ckv, w_gq, w_gk, w_gv, w_gg, lb, lbs], axis=1).astype(BF16)
        wq3 = w_uq[l].reshape(q_rank, heads, nope + rope)
        wq_n = wq3[:, :, :nope].reshape(q_rank, heads * nope)
        x1, x2 = wq3[:, :, nope:nope + half], wq3[:, :, nope + half:]
        zq = jnp.zeros((q_rank, heads, LANES - rope), F32)
        wq_r = jnp.concatenate([x1, x2, zq], axis=2).reshape(q_rank, heads * LANES)
        wq_rs = jnp.concatenate([-x2, x1, zq], axis=2).reshape(q_rank, heads * LANES)
        wq = jnp.concatenate([wq_n, wq_r, wq_rs], axis=1).astype(BF16)
        w_uk2 = w_uk[l].reshape(kv_rank, heads * nope).astype(BF16)
        w_uv2 = w_uv[l].reshape(kv_rank, heads * v_head).astype(BF16)
        w_ukT = jnp.transpose(w_uk[l], (1, 2, 0)).astype(BF16)
        w_uv_h = jnp.transpose(w_uv[l], (1, 0, 2)).astype(BF16)
        walpha = jnp.zeros((LANES, gla_qk), F32).at[rope:rope + gate_rank].set(w_alpha[l]).astype(BF16)
        row = lambda v: v.reshape(1, -1)
        common = (row(norm_attn_pre[l]), win, row(q_norm[l]), row(kv_norm[l]), wq)
        tail = (walpha, row(b_alpha[l]))
        wo_b = w_o[l].astype(BF16)
        w_up_b = w_up[l].astype(BF16)
        w_down_b = w_down[l].astype(BF16)
        gnw = row(gla_norm[l])

        tm = _tile(seq, 256)
        (q, k, v, ckv_p, kr_p, gq, gk, gv, g, gg) = _proj_call(
            yp, ct_p, st_p, common + (w_uk2, w_uv2) + tail, dims, rope, False, tm)
        o_mla = _flash_call(q, k, v, heads, nope + LANES, v_head, _tile(seq, 1024))
        o_gla, gla_new_p = _gla_chunk_call(gq, gk, gv, g, gg, gnw, gla_heads, dk, dv, _tile(seq, 128))
        h_p, a2_p = _post_attn_call(yp, o_mla, o_gla, wo_b, row(norm_attn_post[l]), row(norm_ffn_pre[l]),
                                    _tile(seq, 512))
        yp, tail_g, tail_v = _ffn_prompt_call(a2_p, h_p, w_up_b, conv_w[l], conv_b[l], w_down_b,
                                              row(norm_ffn_post[l]), _tile(seq, 512), _tile(dff, 512))
        conv_new_p = jnp.concatenate([tail_g[-2:], tail_v[-2:]], axis=1)[None]

        (q_abs, q_rope, ckv_s, kr_s, sq, sk, sv, sg, sgg) = _proj_call(
            ys, ct_s, st_s, common + (w_ukT, w_uv2) + tail, dims, rope, True, bs)
        lat_out = _decode_call(page_table, q_abs.reshape(bs, heads, kv_rank), q_rope.reshape(bs, heads, LANES),
                               ckv_s.reshape(bs, 1, kv_rank), kr_s.reshape(bs, 1, rope),
                               cache_kv_latent, cache_rope_t, l, pages_per_step)
        o_mla_s = _uv_call(lat_out.reshape(bs, heads * kv_rank), w_uv_h)
        sh = lambda z, w: z.reshape(bs, gla_heads, w)
        o_gla_s, gla_new_s = _gla_step_call(sh(sq, dk), sh(sk, dk), sh(sg, dk), sh(sv, dv), sh(sgg, dv), gnw,
                                            state_gla, l)
        h_s, a2_s = _post_attn_call(ys, o_mla_s, o_gla_s.reshape(bs, mix_gla), wo_b, row(norm_attn_post[l]),
                                    row(norm_ffn_pre[l]), bs)
        hist_t = jnp.swapaxes(state_ffn_conv[l], 0, 1)
        ys, u_g, u_v = _ffn_sample_call(a2_s, h_s, hist_t, w_up_b, conv_w[l], conv_b[l], w_down_b,
                                        row(norm_ffn_post[l]), _tile(dff, 512))
        conv_new_s = jnp.swapaxes(jnp.stack([hist_t[1], jnp.concatenate([u_g, u_v], axis=1)], axis=0), 0, 1)

        outs["lat_p"].append(ckv_p[None]); outs["rope_p"].append(kr_p[None])
        outs["gla_p"].append(gla_new_p[None].astype(x_prompt.dtype)); outs["conv_p"].append(conv_new_p)
        outs["lat_s"].append(ckv_s[:, None]); outs["rope_s"].append(kr_s[:, None])
        outs["gla_s"].append(gla_new_s.astype(state_gla.dtype)); outs["conv_s"].append(conv_new_s)

    st = lambda name: jnp.stack(outs[name])
    return (yp[None], ys[:, None], st("lat_p"), st("rope_p"), st("gla_p"), st("conv_p"),
            st("lat_s"), st("rope_s"), st("gla_s"), st("conv_s"))
```

```python
import functools

import numpy as np
import jax
import jax.numpy as jnp
from jax import lax
from jax.experimental import pallas as pl
from jax.experimental.pallas import tpu as pltpu

EPS = 1e-6
ROPE_BASE = 10000.0
GATE_NORM = 16.0
LANES = 128
V7X_VMEM_LIMIT_BYTES = 56 * 1024 * 1024

F32 = jnp.float32
BF16 = jnp.bfloat16
NEG_INF = float("-inf")


def _cparams(sem):
    return pltpu.CompilerParams(dimension_semantics=sem, vmem_limit_bytes=V7X_VMEM_LIMIT_BYTES)


def _rms(x, w):
    return x * lax.rsqrt(jnp.mean(x * x, axis=-1, keepdims=True) + EPS) * w


def _dot(a, b):
    return jnp.dot(a, b, preferred_element_type=F32)


def _dot_nt(a, b):
    return lax.dot_general(a, b, (((1,), (1,)), ((), ())), preferred_element_type=F32)


def _dot_tn(a, b):
    return lax.dot_general(a, b, (((0,), (0,)), ((), ())), preferred_element_type=F32)


def _const_spec(shape):
    nd = len(shape)
    return pl.BlockSpec(shape, lambda *_: (0,) * nd, pipeline_mode=pl.Buffered(1))


def _proj_kernel(dims, sample, x_ref, ct_ref, st_ref, wn_ref, win_ref, qn_ref, kvn_ref, wq_ref,
                 wa_ref, wb_ref, walpha_ref, balpha_ref, *outs):
    (q_rank, kv_rank, heads, nope, gla_qk, mix_gla, mla_scale, gla_qscale) = dims
    a = _rms(x_ref[...], wn_ref[...]).astype(BF16)

    off = [0]

    def mm(width):
        lo = off[0]
        off[0] += width
        return _dot(a, win_ref[:, lo:lo + width])

    cq = mm(q_rank)
    ckv = mm(kv_rank)
    gq = mm(gla_qk)
    gk = mm(gla_qk)
    gv = mm(mix_gla)
    gg = mm(mix_gla)
    lb = mm(LANES)
    lbs = mm(LANES)

    ct = ct_ref[...]
    st = st_ref[...]

    cqn = _rms(cq, qn_ref[...]).astype(BF16)
    hn = heads * nope
    q_nope = _dot(cqn, wq_ref[:, 0:hn]) * mla_scale
    q_r = _dot(cqn, wq_ref[:, hn:hn + heads * LANES])
    q_rs = _dot(cqn, wq_ref[:, hn + heads * LANES:hn + 2 * heads * LANES])

    c_kv = _rms(ckv, kvn_ref[...])
    c_kv_b = c_kv.astype(BF16)
    k_rope_blk = lb * ct + lbs * st

    glog = _dot(lb.astype(BF16), walpha_ref[...]) + balpha_ref[...]
    g = (jnp.minimum(glog, 0.0) - jnp.log1p(jnp.exp(-jnp.abs(glog)))) * (1.0 / GATE_NORM)

    if sample:
        qabs_ref, qrope_ref, ckv_ref, kr_ref, gq_ref, gk_ref, gv_ref, g_ref, gg_ref = outs
        for h in range(heads):
            qn_h = q_nope[:, h * nope:(h + 1) * nope].astype(BF16)
            qabs_ref[:, h * kv_rank:(h + 1) * kv_rank] = _dot(qn_h, wa_ref[h]).astype(BF16)
            blk = slice(h * LANES, (h + 1) * LANES)
            qrope_ref[:, blk] = ((q_r[:, blk] * ct + q_rs[:, blk] * st) * mla_scale).astype(BF16)
    else:
        q_ref, k_ref, v_ref, ckv_ref, kr_ref, gq_ref, gk_ref, gv_ref, g_ref, gg_ref = outs
        k_nope = _dot(c_kv_b, wa_ref[...])
        v_ref[...] = _dot(c_kv_b, wb_ref[...]).astype(BF16)
        k_rope_b = k_rope_blk.astype(BF16)
        for h in range(heads):
            blk = slice(h * LANES, (h + 1) * LANES)
            base = h * (nope + LANES)
            q_ref[:, base:base + nope] = q_nope[:, h * nope:(h + 1) * nope].astype(BF16)
            q_ref[:, base + nope:base + nope + LANES] = (
                (q_r[:, blk] * ct + q_rs[:, blk] * st) * mla_scale).astype(BF16)
            k_ref[:, base:base + nope] = k_nope[:, h * nope:(h + 1) * nope].astype(BF16)
            k_ref[:, base + nope:base + nope + LANES] = k_rope_b

    ckv_ref[...] = c_kv
    kr_ref[...] = k_rope_blk[:, 0:kr_ref.shape[1]]
    gq_ref[...] = gq * gla_qscale
    gk_ref[...] = gk
    gv_ref[...] = gv
    g_ref[...] = g
    gg_ref[...] = gg


def _proj_call(x, ct, st, wts, dims, rope, sample, tm):
    (q_rank, kv_rank, heads, nope, gla_qk, mix_gla, _, _) = dims
    m, d = x.shape
    row = lambda w: pl.BlockSpec((tm, w), lambda i: (i, 0))
    in_specs = [row(d), row(LANES), row(LANES)] + [_const_spec(w.shape) for w in wts]
    gla_shapes = [(m, gla_qk), (m, gla_qk), (m, mix_gla), (m, gla_qk), (m, mix_gla)]
    if sample:
        out_shapes = [((m, heads * kv_rank), BF16), ((m, heads * LANES), BF16)]
    else:
        out_shapes = [((m, heads * (nope + LANES)), BF16), ((m, heads * (nope + LANES)), BF16),
                      ((m, heads * nope), BF16)]
    out_shapes += [((m, kv_rank), F32), ((m, rope), F32)] + [(s, F32) for s in gla_shapes]
    return pl.pallas_call(
        functools.partial(_proj_kernel, dims, sample),
        grid=(m // tm,),
        in_specs=in_specs,
        out_specs=[row(s[1]) for s, _ in out_shapes],
        out_shape=[jax.ShapeDtypeStruct(s, dt) for s, dt in out_shapes],
        compiler_params=_cparams(("parallel",)),
        name="proj_sample" if sample else "proj_prompt",
    )(x, ct, st, *wts)


FLASH_SUB = 512


def _flash_kernel(qi_ref, kj_ref, q_ref, k_ref, v_ref, o_ref, m_ref, l_ref, acc_ref):
    t = pl.program_id(1)
    i = qi_ref[t]
    j = kj_ref[t]
    tq, tk = q_ref.shape[0], k_ref.shape[0]

    @pl.when(j == 0)
    def _():
        m_ref[...] = jnp.full(m_ref.shape, NEG_INF, F32)
        l_ref[...] = jnp.zeros(l_ref.shape, F32)
        acc_ref[...] = jnp.zeros(acc_ref.shape, F32)

    def update(masked):
        ts = FLASH_SUB if tk % FLASH_SUB == 0 else tk
        q = q_ref[...]
        m, l, acc = m_ref[...], l_ref[...], acc_ref[...]
        for c0 in range(0, tk, ts):
            s = _dot_nt(q, k_ref[c0:c0 + ts, :])
            if masked:
                rowi = lax.broadcasted_iota(jnp.int32, (tq, ts), 0)
                coli = lax.broadcasted_iota(jnp.int32, (tq, ts), 1) + c0
                s = jnp.where(coli <= rowi, s, NEG_INF)
            m_new = jnp.maximum(m, jnp.max(s, axis=-1, keepdims=True))
            alpha = jnp.exp(m - m_new)
            p = jnp.exp(s - jnp.tile(m_new, (1, ts // LANES)))
            l = alpha * l + jnp.sum(p, axis=-1, keepdims=True)
            acc = alpha * acc + _dot(p.astype(BF16), v_ref[c0:c0 + ts, :])
            m = m_new
        m_ref[...], l_ref[...], acc_ref[...] = m, l, acc

    @pl.when(j < i)
    def _():
        update(False)

    @pl.when(j == i)
    def _():
        update(True)
        o_ref[...] = (acc_ref[...] / l_ref[...]).astype(o_ref.dtype)


def _flash_call(q, k, v, heads, qk_w, v_w, t):
    assert v_w == LANES
    s = q.shape[0]
    n = s // t
    pairs = [(i, j) for i in range(n) for j in range(i + 1)]
    qi = jnp.asarray([p[0] for p in pairs], jnp.int32)
    kj = jnp.asarray([p[1] for p in pairs], jnp.int32)
    grid_spec = pltpu.PrefetchScalarGridSpec(
        num_scalar_prefetch=2,
        grid=(heads, len(pairs)),
        in_specs=[pl.BlockSpec((t, qk_w), lambda h, p, qi, kj: (qi[p], h)),
                  pl.BlockSpec((t, qk_w), lambda h, p, qi, kj: (kj[p], h)),
                  pl.BlockSpec((t, v_w), lambda h, p, qi, kj: (kj[p], h))],
        out_specs=pl.BlockSpec((t, v_w), lambda h, p, qi, kj: (qi[p], h)),
        scratch_shapes=[pltpu.VMEM((t, LANES), F32), pltpu.VMEM((t, LANES), F32),
                        pltpu.VMEM((t, v_w), F32)],
    )
    return pl.pallas_call(
        _flash_kernel,
        grid_spec=grid_spec,
        out_shape=jax.ShapeDtypeStruct((s, heads * v_w), BF16),
        compiler_params=_cparams(("parallel", "arbitrary")),
        name="mla_prompt_flash",
    )(qi, kj, q, k, v)


def _gla_chunk_kernel(heads, dk, dv, q_ref, k_ref, v_ref, g_ref, gg_ref, nw_ref, o_ref, sfin_ref,
                      state_ref):
    c = pl.program_id(0)
    ch = q_ref.shape[0]

    @pl.when(c == 0)
    def _():
        state_ref[...] = jnp.zeros(state_ref.shape, F32)

    rowi = lax.broadcasted_iota(jnp.int32, (ch, ch), 0)
    coli = lax.broadcasted_iota(jnp.int32, (ch, ch), 1)
    causal = coli <= rowi
    tri = causal.astype(F32)
    nw = nw_ref[...]
    for h in range(heads):
        ks = slice(h * dk, (h + 1) * dk)
        vs = slice(h * dv, (h + 1) * dv)
        b = jnp.dot(tri, g_ref[:, ks], preferred_element_type=F32, precision=lax.Precision.HIGHEST)
        b_last = b[ch - 1:ch, :]
        b_mid = b[ch // 2:ch // 2 + 1, :]
        q = q_ref[:, ks]
        qb = (q * jnp.exp(b)).astype(BF16)
        qm = (q * jnp.exp(b - b_mid)).astype(BF16)
        km = (k_ref[:, ks] * jnp.exp(b_mid - b)).astype(BF16)
        kl = (k_ref[:, ks] * jnp.exp(b_last - b)).astype(BF16)
        vb = v_ref[:, vs].astype(BF16)
        state = state_ref[h]
        attn = jnp.where(causal, _dot_nt(qm, km), 0.0)
        o = _dot(qb, state.astype(BF16)) + _dot(attn.astype(BF16), vb)
        decay_col = jnp.transpose(jnp.broadcast_to(jnp.exp(b_last), (dk, dk)))[:, 0:1]
        state_ref[h] = decay_col * state + _dot_tn(kl, vb)
        gate = gg_ref[:, vs]
        o_ref[:, vs] = (_rms(o, nw) * (gate * jax.nn.sigmoid(gate))).astype(o_ref.dtype)

    @pl.when(c == pl.num_programs(0) - 1)
    def _():
        sfin_ref[...] = state_ref[...]


def _gla_chunk_call(gq, gk, gv, g, gg, nw, heads, dk, dv, ch):
    s = gq.shape[0]
    row = lambda w: pl.BlockSpec((ch, w), lambda c: (c, 0))
    return pl.pallas_call(
        functools.partial(_gla_chunk_kernel, heads, dk, dv),
        grid=(s // ch,),
        in_specs=[row(heads * dk), row(heads * dk), row(heads * dv), row(heads * dk), row(heads * dv),
                  pl.BlockSpec((1, dv), lambda c: (0, 0))],
        out_specs=[row(heads * dv), pl.BlockSpec((heads, dk, dv), lambda c: (0, 0, 0))],
        out_shape=[jax.ShapeDtypeStruct((s, heads * dv), BF16),
                   jax.ShapeDtypeStruct((heads, dk, dv), F32)],
        scratch_shapes=[pltpu.VMEM((heads, dk, dv), F32)],
        compiler_params=_cparams(("arbitrary",)),
        name="gla_prompt_chunked",
    )(gq, gk, gv, g, gg, nw)


DECODE_SLOTS = 8
DECODE_AHEAD = 6


def _decode_kernel(pages, n_chunks, layer, pt_ref, qa_ref, qr_ref, cn_ref, kn_ref, lat_hbm, rk_hbm,
                   o_ref, lat_buf, rk_buf, sem):
    b = pl.program_id(0)
    total = pl.num_programs(0) * n_chunks
    heads, kv_rank = o_ref.shape
    pg = lat_hbm.shape[2]
    rope_w = rk_hbm.shape[2]
    reps = kv_rank // LANES

    def chunk_copies(bi, ci, slot):
        cps = []
        for p in range(pages):
            phys = pt_ref[bi, ci * pages + p]
            cps.append(pltpu.make_async_copy(lat_hbm.at[layer, phys], lat_buf.at[slot, pl.ds(p * pg, pg)],
                                             sem.at[0, slot]))
            cps.append(pltpu.make_async_copy(rk_hbm.at[layer, phys], rk_buf.at[slot, p], sem.at[1, slot]))
        return cps

    def start_chunk(bi, ci):
        for n, cp in enumerate(chunk_copies(bi, ci, ci % DECODE_SLOTS)):
            cp.start(priority=(n // 2) % 2)

    @pl.when(b == 0)
    def _():
        for c in range(DECODE_AHEAD):
            start_chunk(0, c)

    qa = qa_ref[...].astype(F32)
    qt = jnp.transpose(jnp.concatenate([qa, jnp.zeros((LANES - heads, kv_rank), F32)], axis=0))
    qr = qr_ref[:, 0:rope_w].astype(F32)
    half = pages * pg // 2

    def latent_scores(c):
        slot = c % DECODE_SLOTS
        for cp in chunk_copies(b, c, slot):
            cp.wait()
        return (_dot(lat_buf[slot, 0:half], qt), _dot(lat_buf[slot, half:], qt))

    m = jnp.full((heads, LANES), NEG_INF, F32)
    l = jnp.zeros((heads, LANES), F32)
    acc = jnp.zeros((heads, kv_rank), F32)
    st = latent_scores(0)
    for c in range(n_chunks):
        slot = c % DECODE_SLOTS
        ahead = c + DECODE_AHEAD
        nb, nc = (b, ahead) if ahead < n_chunks else (b + 1, ahead - n_chunks)

        @pl.when(b * n_chunks + ahead < total)
        def _():
            start_chunk(nb, nc)

        st_next = latent_scores(c + 1) if c + 1 < n_chunks else None
        per_half = pages // 2
        s = jnp.concatenate(
            [jnp.transpose(st[p // per_half][(p % per_half) * pg:(p % per_half + 1) * pg, :])[0:heads, :]
             + _dot(qr, rk_buf[slot, p]) for p in range(pages)], axis=1)
        m_new = jnp.maximum(m, jnp.max(s, axis=-1, keepdims=True))
        alpha = jnp.exp(m - m_new)
        p_un = jnp.exp(s - jnp.tile(m_new, (1, pages * pg // LANES)))
        l = alpha * l + jnp.sum(p_un, axis=-1, keepdims=True)
        acc = jnp.tile(alpha, (1, reps)) * acc + _dot(p_un, lat_buf[slot])
        m = m_new
        st = st_next

    cn = cn_ref[...]
    kn = kn_ref[...]
    s_self = (jnp.sum(qa * cn, axis=-1, keepdims=True)
              + jnp.sum(qr * kn, axis=-1, keepdims=True))
    m_new = jnp.maximum(m, s_self)
    alpha = jnp.exp(m - m_new)
    p_self = jnp.exp(s_self - m_new)
    l_fin = alpha * l + p_self
    acc_fin = jnp.tile(alpha, (1, reps)) * acc + jnp.tile(p_self, (1, reps)) * cn
    o_ref[...] = (acc_fin / jnp.tile(l_fin, (1, reps))).astype(o_ref.dtype)


def _decode_call(page_table, q_abs, q_rope, ckv_new, kr_new, cache_lat, cache_rope_t, layer, pages):
    b, heads, kv_rank = q_abs.shape
    n_pages = page_table.shape[1]
    page = cache_lat.shape[2]
    rope_w = cache_rope_t.shape[2]
    n_chunks = n_pages // pages
    assert n_chunks % DECODE_SLOTS == 0 and DECODE_AHEAD < DECODE_SLOTS - 1 and pages % 2 == 0

    per_b = lambda *shape: pl.BlockSpec((None,) + shape, lambda bi, pt: (bi,) + (0,) * len(shape))
    grid_spec = pltpu.PrefetchScalarGridSpec(
        num_scalar_prefetch=1,
        grid=(b,),
        in_specs=[per_b(heads, kv_rank), per_b(heads, LANES), per_b(1, kv_rank),
                  per_b(1, rope_w), pl.BlockSpec(memory_space=pl.ANY), pl.BlockSpec(memory_space=pl.ANY)],
        out_specs=per_b(heads, kv_rank),
        scratch_shapes=[pltpu.VMEM((DECODE_SLOTS, pages * page, kv_rank), F32),
                        pltpu.VMEM((DECODE_SLOTS, pages, rope_w, page), F32),
                        pltpu.SemaphoreType.DMA((2, DECODE_SLOTS))],
    )
    return pl.pallas_call(
        functools.partial(_decode_kernel, pages, n_chunks, layer),
        grid_spec=grid_spec,
        out_shape=jax.ShapeDtypeStruct((b, heads, kv_rank), BF16),
        compiler_params=_cparams(("arbitrary",)),
        name="mla_paged_decode",
    )(page_table, q_abs, q_rope, ckv_new, kr_new, cache_lat, cache_rope_t)


def _uv_kernel(x_ref, w_ref, o_ref):
    o_ref[...] = _dot(x_ref[...], w_ref[...]).astype(o_ref.dtype)


def _uv_call(lat_out, w_uv_h):
    heads, c, dv = w_uv_h.shape
    b = lat_out.shape[0]
    return pl.pallas_call(
        _uv_kernel,
        grid=(heads,),
        in_specs=[pl.BlockSpec((b, c), lambda h: (0, h)),
                  pl.BlockSpec((None, c, dv), lambda h: (h, 0, 0))],
        out_specs=pl.BlockSpec((b, dv), lambda h: (0, h)),
        out_shape=jax.ShapeDtypeStruct((b, heads * dv), BF16),
        compiler_params=_cparams(("parallel",)),
        name="mla_decode_value_up",
    )(lat_out, w_uv_h)


def _gla_step_kernel(heads, q_ref, k_ref, g_ref, v_ref, gg_ref, nw_ref, s_ref, o_ref, snew_ref):
    bb, _, dk = q_ref.shape
    rows = [r[i] for i in range(bb) for r in (q_ref, k_ref)] + [jnp.exp(g_ref[i]) for i in range(bb)]
    pad = jnp.zeros((dk - 3 * heads * bb, dk), F32)
    cols = jnp.transpose(jnp.concatenate(rows + [pad], axis=0))
    nw = nw_ref[...]
    for i in range(bb):
        for h in range(heads):
            cq, ck, ce = (2 * i) * heads + h, (2 * i + 1) * heads + h, (2 * bb + i) * heads + h
            s_new = cols[:, ce:ce + 1] * s_ref[i, h] + cols[:, ck:ck + 1] * v_ref[i, h:h + 1, :]
            snew_ref[i, h] = s_new
            o = jnp.sum(cols[:, cq:cq + 1] * s_new, axis=0, keepdims=True)
            gate = gg_ref[i, h:h + 1, :]
            o_ref[i, h:h + 1, :] = (_rms(o, nw) * (gate * jax.nn.sigmoid(gate))).astype(o_ref.dtype)


def _gla_step_call(gq, gk, g, gv, gg, nw, state, layer):
    b, heads, dk = gq.shape
    dv = gv.shape[2]
    bb = next(n for n in (8, 4, 2, 1) if b % n == 0 and 3 * heads * n <= dk)
    per_b = lambda w: pl.BlockSpec((bb, heads, w), lambda bi: (bi, 0, 0))
    return pl.pallas_call(
        functools.partial(_gla_step_kernel, heads),
        grid=(b // bb,),
        in_specs=[per_b(dk), per_b(dk), per_b(dk), per_b(dv), per_b(dv),
                  pl.BlockSpec((1, dv), lambda bi: (0, 0)),
                  pl.BlockSpec((None, bb, heads, dk, dv), lambda bi: (layer, bi, 0, 0, 0))],
        out_specs=[per_b(dv), pl.BlockSpec((bb, heads, dk, dv), lambda bi: (bi, 0, 0, 0))],
        out_shape=[jax.ShapeDtypeStruct((b, heads, dv), BF16),
                   jax.ShapeDtypeStruct((b, heads, dk, dv), F32)],
        compiler_params=_cparams(("parallel",)),
        name="gla_sample_step",
    )(gq, gk, g, gv, gg, nw, state)


def _post_attn_kernel(x_ref, om_ref, og_ref, wo_ref, np_ref, nf_ref, h_ref, a_ref):
    nm = om_ref.shape[1]
    mix = _dot(om_ref[...], wo_ref[0:nm, :]) + _dot(og_ref[...], wo_ref[nm:, :])
    h = x_ref[...] + _rms(mix, np_ref[...])
    h_ref[...] = h
    a_ref[...] = _rms(h, nf_ref[...]).astype(a_ref.dtype)


def _post_attn_call(x, o_mla, o_gla, wo, n_post, n_ffn, tm):
    m, d = x.shape
    row = lambda w: pl.BlockSpec((tm, w), lambda i: (i, 0))
    return pl.pallas_call(
        _post_attn_kernel,
        grid=(m // tm,),
        in_specs=[row(d), row(o_mla.shape[1]), row(o_gla.shape[1]), _const_spec(wo.shape),
                  _const_spec(n_post.shape), _const_spec(n_ffn.shape)],
        out_specs=[row(d), row(d)],
        out_shape=[jax.ShapeDtypeStruct((m, d), F32), jax.ShapeDtypeStruct((m, d), BF16)],
        compiler_params=_cparams(("parallel",)),
        name="attn_out_proj",
    )(x, o_mla, o_gla, wo, n_post, n_ffn)


HALO = 8

def _gelu_tanh(x):
    return 0.5 * x * (1.0 + jnp.tanh(np.sqrt(2.0 / np.pi) * (x + 0.044715 * (x * x * x))))


def _ffn_prompt_kernel(a_ref, h_ref, wg_ref, wv_ref, cwg_ref, cwv_ref, cbg_ref, cbv_ref, wd_ref,
                       nw_ref, y_ref, tg_ref, tv_ref, acc_ref, ug_ref, uv_ref, pg_ref, pv_ref, act_refs):
    i = pl.program_id(0)
    j = pl.program_id(1)
    nf = pl.num_programs(1) - 1
    tm = a_ref.shape[0]

    @pl.when(jnp.logical_and(i == 0, j < nf))
    def _():
        pg_ref[j] = jnp.zeros(pg_ref.shape[1:], F32)
        pv_ref[j] = jnp.zeros(pv_ref.shape[1:], F32)

    def up_part(parity):
        a = a_ref[...]

        def conv_half(w_ref, cw_ref, cb_ref, u_ref, tail_ref, prev_ref):
            u_ref[0:HALO, :] = prev_ref[j]
            u = _dot(a, w_ref[...])
            u_ref[HALO:, :] = u
            tail = u[tm - HALO:, :]
            tail_ref[...] = tail
            prev_ref[j] = tail
            return (u_ref[pl.ds(HALO - 2, tm), :] * cw_ref[0:1, :]
                    + u_ref[pl.ds(HALO - 1, tm), :] * cw_ref[1:2, :] + u * cw_ref[2:3, :] + cb_ref[...])

        cg = conv_half(wg_ref, cwg_ref, cbg_ref, ug_ref, tg_ref, pg_ref)
        cv = conv_half(wv_ref, cwv_ref, cbv_ref, uv_ref, tv_ref, pv_ref)
        act_refs[parity] = (_gelu_tanh(cg) * cv).astype(BF16)

    def down_part(parity):
        acc_ref[...] += _dot(act_refs[parity], wd_ref[...])

    @pl.when(j == 0)
    def _():
        acc_ref[...] = jnp.zeros(acc_ref.shape, F32)
        up_part(0)

    for parity in (0, 1):
        @pl.when(jnp.logical_and(jnp.logical_and(j > 0, j < nf), j % 2 == parity))
        def _():
            up_part(parity)
            down_part(1 - parity)

    for parity in (0, 1):
        @pl.when(jnp.logical_and(j == nf, (nf - 1) % 2 == parity))
        def _():
            down_part(parity)
            y_ref[...] = h_ref[...] + _rms(acc_ref[...], nw_ref[...])


def _ffn_prompt_call(a2, h, w_up, conv_w, conv_b, w_down, n_post, tm, tf):
    m, d = h.shape
    dff = w_down.shape[0]
    nf = dff // tf
    cb = conv_b.reshape(1, 2 * dff)
    up = lambda j: jnp.minimum(j, nf - 1)
    down = lambda j: jnp.maximum(j - 1, 0)
    return pl.pallas_call(
        _ffn_prompt_kernel,
        grid=(m // tm, nf + 1),
        in_specs=[pl.BlockSpec((tm, d), lambda i, j: (i, 0)),
                  pl.BlockSpec((tm, d), lambda i, j: (i, 0)),
                  pl.BlockSpec((d, tf), lambda i, j: (0, up(j))),
                  pl.BlockSpec((d, tf), lambda i, j: (0, nf + up(j))),
                  pl.BlockSpec((conv_w.shape[0], tf), lambda i, j: (0, up(j))),
                  pl.BlockSpec((conv_w.shape[0], tf), lambda i, j: (0, nf + up(j))),
                  pl.BlockSpec((1, tf), lambda i, j: (0, up(j))),
                  pl.BlockSpec((1, tf), lambda i, j: (0, nf + up(j))),
                  pl.BlockSpec((tf, d), lambda i, j: (down(j), 0)),
                  pl.BlockSpec((1, d), lambda i, j: (0, 0))],
        out_specs=[pl.BlockSpec((tm, d), lambda i, j: (i, 0)),
                   pl.BlockSpec((HALO, tf), lambda i, j: (i, up(j))),
                   pl.BlockSpec((HALO, tf), lambda i, j: (i, up(j)))],
        out_shape=[jax.ShapeDtypeStruct((m, d), F32),
                   jax.ShapeDtypeStruct((m // tm * HALO, dff), F32),
                   jax.ShapeDtypeStruct((m // tm * HALO, dff), F32)],
        scratch_shapes=[pltpu.VMEM((tm, d), F32), pltpu.VMEM((tm + HALO, tf), F32),
                        pltpu.VMEM((tm + HALO, tf), F32), pltpu.VMEM((nf, HALO, tf), F32),
                        pltpu.VMEM((nf, HALO, tf), F32), pltpu.VMEM((2, tm, tf), BF16)],
        compiler_params=_cparams(("arbitrary", "arbitrary")),
        name="ffn_prompt",
    )(a2, h, w_up, w_up, conv_w, conv_w, cb, cb, w_down, n_post)


def _ffn_sample_kernel(a_ref, h_ref, h0g_ref, h1g_ref, h0v_ref, h1v_ref, wg_ref, wv_ref, cwg_ref, cwv_ref,
                       cbg_ref, cbv_ref, wd_ref, nw_ref, y_ref, ug_ref, uv_ref, acc_ref):
    j = pl.program_id(0)

    @pl.when(j == 0)
    def _():
        acc_ref[...] = jnp.zeros(acc_ref.shape, F32)

    a = a_ref[...]

    def conv_half(w_ref, cw_ref, cb_ref, h0_ref, h1_ref, u_ref):
        u = _dot(a, w_ref[...])
        u_ref[...] = u
        return (h0_ref[...] * cw_ref[0:1, :] + h1_ref[...] * cw_ref[1:2, :] + u * cw_ref[2:3, :]
                + cb_ref[...])

    cg = conv_half(wg_ref, cwg_ref, cbg_ref, h0g_ref, h1g_ref, ug_ref)
    cv = conv_half(wv_ref, cwv_ref, cbv_ref, h0v_ref, h1v_ref, uv_ref)
    act = (_gelu_tanh(cg) * cv).astype(BF16)
    acc_ref[...] += _dot(act, wd_ref[...])

    @pl.when(j == pl.num_programs(0) - 1)
    def _():
        y_ref[...] = h_ref[...] + _rms(acc_ref[...], nw_ref[...])


def _ffn_sample_call(a2, h, hist, w_up, conv_w, conv_b, w_down, n_post, tf):
    b, d = h.shape
    dff = w_down.shape[0]
    nf = dff // tf
    cb = conv_b.reshape(1, 2 * dff)
    full = lambda w: pl.BlockSpec((b, w), lambda j: (0, 0))
    col = lambda base: pl.BlockSpec((b, tf), lambda j: (0, base + j))
    prev = lambda r, base: pl.BlockSpec((None, b, tf), lambda j: (r, 0, base + j))
    return pl.pallas_call(
        _ffn_sample_kernel,
        grid=(nf,),
        in_specs=[full(d), full(d), prev(0, 0), prev(1, 0), prev(0, nf), prev(1, nf),
                  pl.BlockSpec((d, tf), lambda j: (0, j)),
                  pl.BlockSpec((d, tf), lambda j: (0, nf + j)),
                  pl.BlockSpec((conv_w.shape[0], tf), lambda j: (0, j)),
                  pl.BlockSpec((conv_w.shape[0], tf), lambda j: (0, nf + j)),
                  pl.BlockSpec((1, tf), lambda j: (0, j)),
                  pl.BlockSpec((1, tf), lambda j: (0, nf + j)),
                  pl.BlockSpec((tf, d), lambda j: (j, 0)),
                  pl.BlockSpec((1, d), lambda j: (0, 0))],
        out_specs=[full(d), col(0), col(0)],
        out_shape=[jax.ShapeDtypeStruct((b, d), F32), jax.ShapeDtypeStruct((b, dff), F32),
                   jax.ShapeDtypeStruct((b, dff), F32)],
        scratch_shapes=[pltpu.VMEM((b, d), F32)],
        compiler_params=_cparams(("arbitrary",)),
        name="ffn_sample",
    )(a2, h, hist, hist, hist, hist, w_up, w_up, conv_w, conv_w, cb, cb, w_down, n_post)


def _rope_tables(pos, half):
    inv = ROPE_BASE ** (-2.0 * np.arange(half, dtype=np.float64) / (2 * half))
    ang = np.asarray(pos, np.float64)[:, None] * inv[None, :]
    zeros = np.zeros((ang.shape[0], LANES - 2 * half))
    cos, sin = np.cos(ang), np.sin(ang)
    return (jnp.asarray(np.concatenate([cos, cos, zeros], axis=1), F32),
            jnp.asarray(np.concatenate([sin, sin, zeros], axis=1), F32))


def _tile(n, pref):
    return pref if n % pref == 0 else n


def kernel(x_prompt, x_sample, cache_kv_latent, cache_k_rope, state_gla, state_ffn_conv, page_table,
           norm_attn_pre, w_in, q_norm, w_uq, kv_norm, w_uk, w_uv, w_alpha, b_alpha, gla_norm, w_o,
           norm_attn_post, norm_ffn_pre, w_up, conv_w, conv_b, w_down, norm_ffn_post):
    depth = w_in.shape[0]
    _, seq, d = x_prompt.shape
    bs, t_new, _ = x_sample.shape
    assert x_prompt.shape[0] == 1 and t_new == 1
    q_rank = w_uq.shape[1]
    kv_rank, heads, nope = w_uk.shape[1:]
    v_head = w_uv.shape[3]
    rope = cache_k_rope.shape[3]
    half = rope // 2
    assert w_uq.shape[2] == heads * (nope + rope) and nope == LANES and v_head == LANES and rope <= LANES // 2
    _, _, gla_heads, dk, dv = state_gla.shape
    gla_qk, mix_gla = gla_heads * dk, gla_heads * dv
    gate_rank = w_alpha.shape[1]
    assert rope + gate_rank <= LANES
    dff = w_down.shape[1]
    assert conv_w.shape[1] == 3
    past = page_table.shape[1] * cache_kv_latent.shape[2]
    mla_scale = float((nope + rope) ** -0.5)
    dims = (q_rank, kv_rank, heads, nope, gla_qk, mix_gla, mla_scale, float(dk ** -0.5))

    ct_p, st_p = _rope_tables(np.arange(seq), half)
    ct_s, st_s = _rope_tables(np.full((bs,), past), half)
    cache_rope_t = jnp.swapaxes(cache_k_rope, 2, 3)
    n_pages = page_table.shape[1]
    pages_per_step = next(p for p in (16, 8, 4, 2) if n_pages % (p * DECODE_SLOTS) == 0)

    yp, ys = x_prompt[0], x_sample[:, 0]
    outs = {k: [] for k in ("lat_p", "rope_p", "gla_p", "conv_p", "lat_s", "rope_s", "gla_s", "conv_s")}
    for l in range(depth):
        sizes = (q_rank, kv_rank, rope, gla_qk, gla_qk, mix_gla, gate_rank, mix_gla)
        idx = np.cumsum(sizes)[:-1].tolist()
        w_cq, w_ckv, w_kr, w_gq, w_gk, w_gv, w_ga, w_gg = jnp.split(w_in[l], idx, axis=1)
        zc = lambda n: jnp.zeros((d, n), F32)
        lb = jnp.concatenate([w_kr, w_ga, zc(LANES - rope - gate_rank)], axis=1)
        lbs = jnp.concatenate([-w_kr[:, half:], w_kr[:, :half], zc(LANES - rope)], axis=1)
        win = jnp.concatenate([w_cq, w_ckv, w_gq, w_gk, w_gv, w_gg, lb, lbs], axis=1).astype(BF16)
        wq3 = w_uq[l].reshape(q_rank, heads, nope + rope)
        wq_n = wq3[:, :, :nope].reshape(q_rank, heads * nope)
        x1, x2 = wq3[:, :, nope:nope + half], wq3[:, :, nope + half:]
        zq = jnp.zeros((q_rank, heads, LANES - rope), F32)
        wq_r = jnp.concatenate([x1, x2, zq], axis=2).reshape(q_rank, heads * LANES)
        wq_rs = jnp.concatenate([-x2, x1, zq], axis=2).reshape(q_rank, heads * LANES)
        wq = jnp.concatenate([wq_n, wq_r, wq_rs], axis=1).astype(BF16)
        w_uk2 = w_uk[l].reshape(kv_rank, heads * nope).astype(BF16)
        w_uv2 = w_uv[l].reshape(kv_rank, heads * v_head).astype(BF16)
        w_ukT = jnp.transpose(w_uk[l], (1, 2, 0)).astype(BF16)
        w_uv_h = jnp.transpose(w_uv[l], (1, 0, 2)).astype(BF16)
        walpha = jnp.zeros((LANES, gla_qk), F32).at[rope:rope + gate_rank].set(w_alpha[l]).astype(BF16)
        row = lambda v: v.reshape(1, -1)
        common = (row(norm_attn_pre[l]), win, row(q_norm[l]), row(kv_norm[l]), wq)
        tail = (walpha, row(b_alpha[l]))
        wo_b = w_o[l].astype(BF16)
        w_up_b = w_up[l].astype(BF16)
        w_down_b = w_down[l].astype(BF16)
        gnw = row(gla_norm[l])

        tm = _tile(seq, 256)
        (q, k, v, ckv_p, kr_p, gq, gk, gv, g, gg) = _proj_call(
            yp, ct_p, st_p, common + (w_uk2, w_uv2) + tail, dims, rope, False, tm)
        o_mla = _flash_call(q, k, v, heads, nope + LANES, v_head, _tile(seq, 1024))
        o_gla, gla_new_p = _gla_chunk_call(gq, gk, gv, g, gg, gnw, gla_heads, dk, dv, _tile(seq, 128))
        h_p, a2_p = _post_attn_call(yp, o_mla, o_gla, wo_b, row(norm_attn_post[l]), row(norm_ffn_pre[l]),
                                    _tile(seq, 512))
        yp, tail_g, tail_v = _ffn_prompt_call(a2_p, h_p, w_up_b, conv_w[l], conv_b[l], w_down_b,
                                              row(norm_ffn_post[l]), _tile(seq, 512), _tile(dff, 512))
        conv_new_p = jnp.concatenate([tail_g[-2:], tail_v[-2:]], axis=1)[None]

        (q_abs, q_rope, ckv_s, kr_s, sq, sk, sv, sg, sgg) = _proj_call(
            ys, ct_s, st_s, common + (w_ukT, w_uv2) + tail, dims, rope, True, bs)
        lat_out = _decode_call(page_table, q_abs.reshape(bs, heads, kv_rank), q_rope.reshape(bs, heads, LANES),
                               ckv_s.reshape(bs, 1, kv_rank), kr_s.reshape(bs, 1, rope),
                               cache_kv_latent, cache_rope_t, l, pages_per_step)
        o_mla_s = _uv_call(lat_out.reshape(bs, heads * kv_rank), w_uv_h)
        sh = lambda z, w: z.reshape(bs, gla_heads, w)
        o_gla_s, gla_new_s = _gla_step_call(sh(sq, dk), sh(sk, dk), sh(sg, dk), sh(sv, dv), sh(sgg, dv), gnw,
                                            state_gla, l)
        h_s, a2_s = _post_attn_call(ys, o_mla_s, o_gla_s.reshape(bs, mix_gla), wo_b, row(norm_attn_post[l]),
                                    row(norm_ffn_pre[l]), bs)
        hist_t = jnp.swapaxes(state_ffn_conv[l], 0, 1)
        ys, u_g, u_v = _ffn_sample_call(a2_s, h_s, hist_t, w_up_b, conv_w[l], conv_b[l], w_down_b,
                                        row(norm_ffn_post[l]), _tile(dff, 512))
        conv_new_s = jnp.swapaxes(jnp.stack([hist_t[1], jnp.concatenate([u_g, u_v], axis=1)], axis=0), 0, 1)

        outs["lat_p"].append(ckv_p[None]); outs["rope_p"].append(kr_p[None])
        outs["gla_p"].append(gla_new_p[None].astype(x_prompt.dtype)); outs["conv_p"].append(conv_new_p)
        outs["lat_s"].append(ckv_s[:, None]); outs["rope_s"].append(kr_s[:, None])
        outs["gla_s"].append(gla_new_s.astype(state_gla.dtype)); outs["conv_s"].append(conv_new_s)

    st = lambda name: jnp.stack(outs[name])
    return (yp[None], ys[:, None], st("lat_p"), st("rope_p"), st("gla_p"), st("conv_p"),
            st("lat_s"), st("rope_s"), st("gla_s"), st("conv_s"))
```

```python
import functools

import numpy as np
import jax
import jax.numpy as jnp
from jax import lax
from jax.experimental import pallas as pl
from jax.experimental.pallas import tpu as pltpu

EPS = 1e-6
ROPE_BASE = 10000.0
GATE_NORM = 16.0
LANES = 128
V7X_VMEM_LIMIT_BYTES = 56 * 1024 * 1024

F32 = jnp.float32
BF16 = jnp.bfloat16
NEG_INF = float("-inf")


def _cparams(sem):
    return pltpu.CompilerParams(dimension_semantics=sem, vmem_limit_bytes=V7X_VMEM_LIMIT_BYTES)


def _rms(x, w):
    return x * lax.rsqrt(jnp.mean(x * x, axis=-1, keepdims=True) + EPS) * w


def _dot(a, b):
    return jnp.dot(a, b, preferred_element_type=F32)


def _dot_nt(a, b):
    return lax.dot_general(a, b, (((1,), (1,)), ((), ())), preferred_element_type=F32)


def _dot_tn(a, b):
    return lax.dot_general(a, b, (((0,), (0,)), ((), ())), preferred_element_type=F32)


def _const_spec(shape):
    nd = len(shape)
    return pl.BlockSpec(shape, lambda *_: (0,) * nd, pipeline_mode=pl.Buffered(1))


REGROUP_WIDTH = 512
SUBLANES = 8


def _regroup_kernel(st_ref, x_ref, o_ref):
    o_ref[...] = jnp.transpose(x_ref[...]).astype(o_ref.dtype)


def _regroup_call(w_t, starts, width):
    cin, r = w_t.shape
    assert all(s % SUBLANES == 0 and s + width <= cin for s in starts)
    n = len(starts)
    grid_spec = pltpu.PrefetchScalarGridSpec(
        num_scalar_prefetch=1,
        grid=(n,),
        in_specs=[pl.BlockSpec((pl.Element(width), pl.Element(r)), lambda g, st: (st[g] * SUBLANES, 0))],
        out_specs=pl.BlockSpec((r, width), lambda g, st: (0, g)),
    )
    return pl.pallas_call(
        _regroup_kernel,
        grid_spec=grid_spec,
        out_shape=jax.ShapeDtypeStruct((r, n * width), BF16),
        compiler_params=_cparams(("parallel",)),
        name="regroup_w_in",
    )(jnp.asarray([s // SUBLANES for s in starts], jnp.int32), w_t)


def _proj_kernel(dims, sample, x_ref, ct_ref, st_ref, wn_ref, win_ref, wtail_ref, qn_ref, kvn_ref, wq_ref,
                 wa_ref, wb_ref, walpha_ref, balpha_ref, *outs):
    (q_rank, kv_rank, heads, nope, gla_qk, mix_gla, mla_scale, gla_qscale) = dims
    a = _rms(x_ref[...], wn_ref[...]).astype(BF16)

    off = [0]

    def mm(width):
        lo = off[0]
        off[0] += width
        return _dot(a, win_ref[:, lo:lo + width])

    cq = mm(q_rank)
    ckv = mm(kv_rank)
    gq = mm(gla_qk)
    gk = mm(gla_qk)
    gv = mm(mix_gla)
    gg = mm(mix_gla)
    lb = _dot(a, wtail_ref[:, 0:LANES])
    lbs = _dot(a, wtail_ref[:, LANES:2 * LANES])

    ct = ct_ref[...]
    st = st_ref[...]

    cqn = _rms(cq, qn_ref[...]).astype(BF16)
    hn = heads * nope
    q_nope = _dot(cqn, wq_ref[:, 0:hn]) * mla_scale
    q_r = _dot(cqn, wq_ref[:, hn:hn + heads * LANES])
    q_rs = _dot(cqn, wq_ref[:, hn + heads * LANES:hn + 2 * heads * LANES])

    c_kv = _rms(ckv, kvn_ref[...])
    c_kv_b = c_kv.astype(BF16)
    k_rope_blk = lb * ct + lbs * st

    glog = _dot(lb.astype(BF16), walpha_ref[...]) + balpha_ref[...]
    g = (jnp.minimum(glog, 0.0) - jnp.log1p(jnp.exp(-jnp.abs(glog)))) * (1.0 / GATE_NORM)

    if sample:
        qabs_ref, qrope_ref, ckv_ref, kr_ref, gq_ref, gk_ref, gv_ref, g_ref, gg_ref = outs
        for h in range(heads):
            qn_h = q_nope[:, h * nope:(h + 1) * nope].astype(BF16)
            qabs_ref[:, h * kv_rank:(h + 1) * kv_rank] = _dot(qn_h, wa_ref[h]).astype(BF16)
            blk = slice(h * LANES, (h + 1) * LANES)
            qrope_ref[:, blk] = ((q_r[:, blk] * ct + q_rs[:, blk] * st) * mla_scale).astype(BF16)
    else:
        q_ref, k_ref, v_ref, ckv_ref, kr_ref, gq_ref, gk_ref, gv_ref, g_ref, gg_ref = outs
        k_nope = _dot(c_kv_b, wa_ref[...])
        v_ref[...] = _dot(c_kv_b, wb_ref[...]).astype(BF16)
        k_rope_b = k_rope_blk.astype(BF16)
        for h in range(heads):
            blk = slice(h * LANES, (h + 1) * LANES)
            base = h * (nope + LANES)
            q_ref[:, base:base + nope] = q_nope[:, h * nope:(h + 1) * nope].astype(BF16)
            q_ref[:, base + nope:base + nope + LANES] = (
                (q_r[:, blk] * ct + q_rs[:, blk] * st) * mla_scale).astype(BF16)
            k_ref[:, base:base + nope] = k_nope[:, h * nope:(h + 1) * nope].astype(BF16)
            k_ref[:, base + nope:base + nope + LANES] = k_rope_b

    ckv_ref[...] = c_kv
    kr_ref[...] = k_rope_blk[:, 0:kr_ref.shape[1]]
    gq_ref[...] = gq * gla_qscale
    gk_ref[...] = gk
    gv_ref[...] = gv
    g_ref[...] = g
    gg_ref[...] = gg


def _proj_call(x, ct, st, wts, dims, rope, sample, tm):
    (q_rank, kv_rank, heads, nope, gla_qk, mix_gla, _, _) = dims
    m, d = x.shape
    row = lambda w: pl.BlockSpec((tm, w), lambda i: (i, 0))
    in_specs = [row(d), row(LANES), row(LANES)] + [_const_spec(w.shape) for w in wts]
    gla_shapes = [(m, gla_qk), (m, gla_qk), (m, mix_gla), (m, gla_qk), (m, mix_gla)]
    if sample:
        out_shapes = [((m, heads * kv_rank), BF16), ((m, heads * LANES), BF16)]
    else:
        out_shapes = [((m, heads * (nope + LANES)), BF16), ((m, heads * (nope + LANES)), BF16),
                      ((m, heads * nope), BF16)]
    out_shapes += [((m, kv_rank), F32), ((m, rope), F32)] + [(s, F32) for s in gla_shapes]
    return pl.pallas_call(
        functools.partial(_proj_kernel, dims, sample),
        grid=(m // tm,),
        in_specs=in_specs,
        out_specs=[row(s[1]) for s, _ in out_shapes],
        out_shape=[jax.ShapeDtypeStruct(s, dt) for s, dt in out_shapes],
        compiler_params=_cparams(("parallel",)),
        name="proj_sample" if sample else "proj_prompt",
    )(x, ct, st, *wts)


FLASH_SUB = 512


def _flash_kernel(qi_ref, kj_ref, q_ref, k_ref, v_ref, o_ref, m_ref, l_ref, acc_ref):
    t = pl.program_id(1)
    i = qi_ref[t]
    j = kj_ref[t]
    tq, tk = q_ref.shape[0], k_ref.shape[0]

    @pl.when(j == 0)
    def _():
        m_ref[...] = jnp.full(m_ref.shape, NEG_INF, F32)
        l_ref[...] = jnp.zeros(l_ref.shape, F32)
        acc_ref[...] = jnp.zeros(acc_ref.shape, F32)

    def update(masked):
        ts = FLASH_SUB if tk % FLASH_SUB == 0 else tk
        q = q_ref[...]
        m, l, acc = m_ref[...], l_ref[...], acc_ref[...]
        for c0 in range(0, tk, ts):
            r0 = c0 if masked else 0
            s = _dot_nt(q[r0:], k_ref[c0:c0 + ts, :])
            if masked:
                rowi = lax.broadcasted_iota(jnp.int32, (tq - r0, ts), 0) + r0
                coli = lax.broadcasted_iota(jnp.int32, (tq - r0, ts), 1) + c0
                s = jnp.where(coli <= rowi, s, NEG_INF)
            m_new = jnp.maximum(m[r0:], jnp.max(s, axis=-1, keepdims=True))
            alpha = jnp.exp(m[r0:] - m_new)
            p = jnp.exp(s - jnp.tile(m_new, (1, ts // LANES)))
            l_new = alpha * l[r0:] + jnp.sum(p, axis=-1, keepdims=True)
            acc_new = alpha * acc[r0:] + _dot(p.astype(BF16), v_ref[c0:c0 + ts, :])
            keep = lambda old, new: jnp.concatenate([old[:r0], new], axis=0) if r0 else new
            m, l, acc = keep(m, m_new), keep(l, l_new), keep(acc, acc_new)
        m_ref[...], l_ref[...], acc_ref[...] = m, l, acc

    @pl.when(j < i)
    def _():
        update(False)

    @pl.when(j == i)
    def _():
        update(True)
        o_ref[...] = (acc_ref[...] / l_ref[...]).astype(o_ref.dtype)


def _flash_call(q, k, v, heads, qk_w, v_w, t):
    assert v_w == LANES
    s = q.shape[0]
    n = s // t
    pairs = [(i, j) for i in range(n) for j in range(i + 1)]
    qi = jnp.asarray([p[0] for p in pairs], jnp.int32)
    kj = jnp.asarray([p[1] for p in pairs], jnp.int32)
    grid_spec = pltpu.PrefetchScalarGridSpec(
        num_scalar_prefetch=2,
        grid=(heads, len(pairs)),
        in_specs=[pl.BlockSpec((t, qk_w), lambda h, p, qi, kj: (qi[p], h)),
                  pl.BlockSpec((t, qk_w), lambda h, p, qi, kj: (kj[p], h)),
                  pl.BlockSpec((t, v_w), lambda h, p, qi, kj: (kj[p], h))],
        out_specs=pl.BlockSpec((t, v_w), lambda h, p, qi, kj: (qi[p], h)),
        scratch_shapes=[pltpu.VMEM((t, LANES), F32), pltpu.VMEM((t, LANES), F32),
                        pltpu.VMEM((t, v_w), F32)],
    )
    return pl.pallas_call(
        _flash_kernel,
        grid_spec=grid_spec,
        out_shape=jax.ShapeDtypeStruct((s, heads * v_w), BF16),
        compiler_params=_cparams(("parallel", "arbitrary")),
        name="mla_prompt_flash",
    )(qi, kj, q, k, v)


def _gla_chunk_kernel(heads, dk, dv, q_ref, k_ref, v_ref, g_ref, gg_ref, nw_ref, o_ref, sfin_ref,
                      state_ref):
    c = pl.program_id(0)
    ch = q_ref.shape[0]

    @pl.when(c == 0)
    def _():
        state_ref[...] = jnp.zeros(state_ref.shape, F32)

    rowi = lax.broadcasted_iota(jnp.int32, (ch, ch), 0)
    coli = lax.broadcasted_iota(jnp.int32, (ch, ch), 1)
    causal = coli <= rowi
    tri = causal.astype(F32)
    nw = nw_ref[...]
    for h in range(heads):
        ks = slice(h * dk, (h + 1) * dk)
        vs = slice(h * dv, (h + 1) * dv)
        b = jnp.dot(tri, g_ref[:, ks], preferred_element_type=F32, precision=lax.Precision.HIGHEST)
        b_last = b[ch - 1:ch, :]
        b_mid = b[ch // 2:ch // 2 + 1, :]
        q = q_ref[:, ks]
        qb = (q * jnp.exp(b)).astype(BF16)
        qm = (q * jnp.exp(b - b_mid)).astype(BF16)
        km = (k_ref[:, ks] * jnp.exp(b_mid - b)).astype(BF16)
        kl = (k_ref[:, ks] * jnp.exp(b_last - b)).astype(BF16)
        vb = v_ref[:, vs].astype(BF16)
        state = state_ref[h]
        attn = jnp.where(causal, _dot_nt(qm, km), 0.0)
        o = _dot(qb, state.astype(BF16)) + _dot(attn.astype(BF16), vb)
        decay_col = jnp.transpose(jnp.broadcast_to(jnp.exp(b_last), (dk, dk)))[:, 0:1]
        state_ref[h] = decay_col * state + _dot_tn(kl, vb)
        gate = gg_ref[:, vs]
        o_ref[:, vs] = (_rms(o, nw) * (gate * jax.nn.sigmoid(gate))).astype(o_ref.dtype)

    @pl.when(c == pl.num_programs(0) - 1)
    def _():
        sfin_ref[...] = state_ref[...]


def _gla_chunk_call(gq, gk, gv, g, gg, nw, heads, dk, dv, ch):
    s = gq.shape[0]
    row = lambda w: pl.BlockSpec((ch, w), lambda c: (c, 0))
    return pl.pallas_call(
        functools.partial(_gla_chunk_kernel, heads, dk, dv),
        grid=(s // ch,),
        in_specs=[row(heads * dk), row(heads * dk), row(heads * dv), row(heads * dk), row(heads * dv),
                  pl.BlockSpec((1, dv), lambda c: (0, 0))],
        out_specs=[row(heads * dv), pl.BlockSpec((heads, dk, dv), lambda c: (0, 0, 0))],
        out_shape=[jax.ShapeDtypeStruct((s, heads * dv), BF16),
                   jax.ShapeDtypeStruct((heads, dk, dv), F32)],
        scratch_shapes=[pltpu.VMEM((heads, dk, dv), F32)],
        compiler_params=_cparams(("arbitrary",)),
        name="gla_prompt_chunked",
    )(gq, gk, gv, g, gg, nw)


DECODE_SLOTS = 8
DECODE_AHEAD = 6


def _decode_kernel(pages, n_chunks, layer, pt_ref, qa_ref, qr_ref, cn_ref, kn_ref, lat_hbm, rk_hbm,
                   o_ref, lat_buf, rk_buf, sem):
    b = pl.program_id(0)
    total = pl.num_programs(0) * n_chunks
    heads, kv_rank = o_ref.shape
    pg = lat_hbm.shape[2]
    rope_w = rk_hbm.shape[2]
    reps = kv_rank // LANES

    def chunk_copies(bi, ci, slot):
        cps = []
        for p in range(pages):
            phys = pt_ref[bi, ci * pages + p]
            cps.append(pltpu.make_async_copy(lat_hbm.at[layer, phys], lat_buf.at[slot, pl.ds(p * pg, pg)],
                                             sem.at[0, slot]))
            cps.append(pltpu.make_async_copy(rk_hbm.at[layer, phys], rk_buf.at[slot, p], sem.at[1, slot]))
        return cps

    def start_chunk(bi, ci):
        for n, cp in enumerate(chunk_copies(bi, ci, ci % DECODE_SLOTS)):
            cp.start(priority=(n // 2) % 2)

    @pl.when(b == 0)
    def _():
        for c in range(DECODE_AHEAD):
            start_chunk(0, c)

    qa = qa_ref[...].astype(F32)
    qt = jnp.transpose(jnp.concatenate([qa, jnp.zeros((LANES - heads, kv_rank), F32)], axis=0))
    qr = qr_ref[:, 0:rope_w].astype(F32)
    half = pages * pg // 2

    def latent_scores(c):
        slot = c % DECODE_SLOTS
        for cp in chunk_copies(b, c, slot):
            cp.wait()
        return (_dot(lat_buf[slot, 0:half], qt), _dot(lat_buf[slot, half:], qt))

    m = jnp.full((heads, LANES), NEG_INF, F32)
    l = jnp.zeros((heads, LANES), F32)
    acc = jnp.zeros((heads, kv_rank), F32)
    st = latent_scores(0)
    for c in range(n_chunks):
        slot = c % DECODE_SLOTS
        ahead = c + DECODE_AHEAD
        nb, nc = (b, ahead) if ahead < n_chunks else (b + 1, ahead - n_chunks)

        @pl.when(b * n_chunks + ahead < total)
        def _():
            start_chunk(nb, nc)

        st_next = latent_scores(c + 1) if c + 1 < n_chunks else None
        per_half = pages // 2
        s = jnp.concatenate(
            [jnp.transpose(st[p // per_half][(p % per_half) * pg:(p % per_half + 1) * pg, :])[0:heads, :]
             + _dot(qr, rk_buf[slot, p]) for p in range(pages)], axis=1)
        m_new = jnp.maximum(m, jnp.max(s, axis=-1, keepdims=True))
        alpha = jnp.exp(m - m_new)
        p_un = jnp.exp(s - jnp.tile(m_new, (1, pages * pg // LANES)))
        l = alpha * l + jnp.sum(p_un, axis=-1, keepdims=True)
        acc = jnp.tile(alpha, (1, reps)) * acc + _dot(p_un, lat_buf[slot])
        m = m_new
        st = st_next

    cn = cn_ref[...]
    kn = kn_ref[...]
    s_self = (jnp.sum(qa * cn, axis=-1, keepdims=True)
              + jnp.sum(qr * kn, axis=-1, keepdims=True))
    m_new = jnp.maximum(m, s_self)
    alpha = jnp.exp(m - m_new)
    p_self = jnp.exp(s_self - m_new)
    l_fin = alpha * l + p_self
    acc_fin = jnp.tile(alpha, (1, reps)) * acc + jnp.tile(p_self, (1, reps)) * cn
    o_ref[...] = (acc_fin / jnp.tile(l_fin, (1, reps))).astype(o_ref.dtype)


def _decode_call(page_table, q_abs, q_rope, ckv_new, kr_new, cache_lat, cache_rope_t, layer, pages):
    b, heads, kv_rank = q_abs.shape
    n_pages = page_table.shape[1]
    page = cache_lat.shape[2]
    rope_w = cache_rope_t.shape[2]
    n_chunks = n_pages // pages
    assert n_chunks % DECODE_SLOTS == 0 and DECODE_AHEAD < DECODE_SLOTS - 1 and pages % 2 == 0

    per_b = lambda *shape: pl.BlockSpec((None,) + shape, lambda bi, pt: (bi,) + (0,) * len(shape))
    grid_spec = pltpu.PrefetchScalarGridSpec(
        num_scalar_prefetch=1,
        grid=(b,),
        in_specs=[per_b(heads, kv_rank), per_b(heads, LANES), per_b(1, kv_rank),
                  per_b(1, rope_w), pl.BlockSpec(memory_space=pl.ANY), pl.BlockSpec(memory_space=pl.ANY)],
        out_specs=per_b(heads, kv_rank),
        scratch_shapes=[pltpu.VMEM((DECODE_SLOTS, pages * page, kv_rank), F32),
                        pltpu.VMEM((DECODE_SLOTS, pages, rope_w, page), F32),
                        pltpu.SemaphoreType.DMA((2, DECODE_SLOTS))],
    )
    return pl.pallas_call(
        functools.partial(_decode_kernel, pages, n_chunks, layer),
        grid_spec=grid_spec,
        out_shape=jax.ShapeDtypeStruct((b, heads, kv_rank), BF16),
        compiler_params=_cparams(("arbitrary",)),
        name="mla_paged_decode",
    )(page_table, q_abs, q_rope, ckv_new, kr_new, cache_lat, cache_rope_t)


def _uv_kernel(x_ref, w_ref, o_ref):
    o_ref[...] = _dot(x_ref[...], w_ref[...]).astype(o_ref.dtype)


def _uv_call(lat_out, w_uv_h):
    heads, c, dv = w_uv_h.shape
    b = lat_out.shape[0]
    return pl.pallas_call(
        _uv_kernel,
        grid=(heads,),
        in_specs=[pl.BlockSpec((b, c), lambda h: (0, h)),
                  pl.BlockSpec((None, c, dv), lambda h: (h, 0, 0))],
        out_specs=pl.BlockSpec((b, dv), lambda h: (0, h)),
        out_shape=jax.ShapeDtypeStruct((b, heads * dv), BF16),
        compiler_params=_cparams(("parallel",)),
        name="mla_decode_value_up",
    )(lat_out, w_uv_h)


def _gla_step_kernel(heads, q_ref, k_ref, g_ref, v_ref, gg_ref, nw_ref, s_ref, o_ref, snew_ref):
    bb, _, dk = q_ref.shape
    rows = [r[i] for i in range(bb) for r in (q_ref, k_ref)] + [jnp.exp(g_ref[i]) for i in range(bb)]
    pad = jnp.zeros((dk - 3 * heads * bb, dk), F32)
    cols = jnp.transpose(jnp.concatenate(rows + [pad], axis=0))
    nw = nw_ref[...]
    for i in range(bb):
        for h in range(heads):
            cq, ck, ce = (2 * i) * heads + h, (2 * i + 1) * heads + h, (2 * bb + i) * heads + h
            s_new = cols[:, ce:ce + 1] * s_ref[i, h] + cols[:, ck:ck + 1] * v_ref[i, h:h + 1, :]
            snew_ref[i, h] = s_new
            o = jnp.sum(cols[:, cq:cq + 1] * s_new, axis=0, keepdims=True)
            gate = gg_ref[i, h:h + 1, :]
            o_ref[i, h:h + 1, :] = (_rms(o, nw) * (gate * jax.nn.sigmoid(gate))).astype(o_ref.dtype)


def _gla_step_call(gq, gk, g, gv, gg, nw, state, layer):
    b, heads, dk = gq.shape
    dv = gv.shape[2]
    bb = next(n for n in (8, 4, 2, 1) if b % n == 0 and 3 * heads * n <= dk)
    per_b = lambda w: pl.BlockSpec((bb, heads, w), lambda bi: (bi, 0, 0))
    return pl.pallas_call(
        functools.partial(_gla_step_kernel, heads),
        grid=(b // bb,),
        in_specs=[per_b(dk), per_b(dk), per_b(dk), per_b(dv), per_b(dv),
                  pl.BlockSpec((1, dv), lambda bi: (0, 0)),
                  pl.BlockSpec((None, bb, heads, dk, dv), lambda bi: (layer, bi, 0, 0, 0))],
        out_specs=[per_b(dv), pl.BlockSpec((bb, heads, dk, dv), lambda bi: (bi, 0, 0, 0))],
        out_shape=[jax.ShapeDtypeStruct((b, heads, dv), BF16),
                   jax.ShapeDtypeStruct((b, heads, dk, dv), F32)],
        compiler_params=_cparams(("parallel",)),
        name="gla_sample_step",
    )(gq, gk, g, gv, gg, nw, state)


def _post_attn_kernel(x_ref, om_ref, og_ref, wo_ref, np_ref, nf_ref, h_ref, a_ref):
    nm = om_ref.shape[1]
    mix = _dot(om_ref[...], wo_ref[0:nm, :]) + _dot(og_ref[...], wo_ref[nm:, :])
    h = x_ref[...] + _rms(mix, np_ref[...])
    h_ref[...] = h
    a_ref[...] = _rms(h, nf_ref[...]).astype(a_ref.dtype)


def _post_attn_call(x, o_mla, o_gla, wo, n_post, n_ffn, tm):
    m, d = x.shape
    row = lambda w: pl.BlockSpec((tm, w), lambda i: (i, 0))
    return pl.pallas_call(
        _post_attn_kernel,
        grid=(m // tm,),
        in_specs=[row(d), row(o_mla.shape[1]), row(o_gla.shape[1]), _const_spec(wo.shape),
                  _const_spec(n_post.shape), _const_spec(n_ffn.shape)],
        out_specs=[row(d), row(d)],
        out_shape=[jax.ShapeDtypeStruct((m, d), F32), jax.ShapeDtypeStruct((m, d), BF16)],
        compiler_params=_cparams(("parallel",)),
        name="attn_out_proj",
    )(x, o_mla, o_gla, wo, n_post, n_ffn)


HALO = 8

def _gelu_tanh(x):
    return 0.5 * x * (1.0 + jnp.tanh(np.sqrt(2.0 / np.pi) * (x + 0.044715 * (x * x * x))))


def _ffn_prompt_kernel(a_ref, h_ref, wg_ref, wv_ref, cwg_ref, cwv_ref, cbg_ref, cbv_ref, wd_ref,
                       nw_ref, y_ref, tg_ref, tv_ref, acc_ref, ug_ref, uv_ref, pg_ref, pv_ref):
    i = pl.program_id(0)
    j = pl.program_id(1)
    tm = a_ref.shape[0]

    @pl.when(j == 0)
    def _():
        acc_ref[...] = jnp.zeros(acc_ref.shape, F32)

    @pl.when(i == 0)
    def _():
        pg_ref[j] = jnp.zeros(pg_ref.shape[1:], F32)
        pv_ref[j] = jnp.zeros(pv_ref.shape[1:], F32)

    a = a_ref[...]

    def conv_half(w_ref, cw_ref, cb_ref, u_ref, tail_ref, prev_ref):
        u_ref[0:HALO, :] = prev_ref[j]
        u = _dot(a, w_ref[...])
        u_ref[HALO:, :] = u
        tail = u[tm - HALO:, :]
        tail_ref[...] = tail
        prev_ref[j] = tail
        return (u_ref[pl.ds(HALO - 2, tm), :] * cw_ref[0:1, :] + u_ref[pl.ds(HALO - 1, tm), :] * cw_ref[1:2, :]
                + u * cw_ref[2:3, :] + cb_ref[...])

    cg = conv_half(wg_ref, cwg_ref, cbg_ref, ug_ref, tg_ref, pg_ref)
    cv = conv_half(wv_ref, cwv_ref, cbv_ref, uv_ref, tv_ref, pv_ref)
    act = (_gelu_tanh(cg) * cv).astype(BF16)
    acc_ref[...] += _dot(act, wd_ref[...])

    @pl.when(j == pl.num_programs(1) - 1)
    def _():
        y_ref[...] = h_ref[...] + _rms(acc_ref[...], nw_ref[...])


def _ffn_prompt_call(a2, h, w_up, conv_w, conv_b, w_down, n_post, tm, tf):
    m, d = h.shape
    dff = w_down.shape[0]
    nf = dff // tf
    cb = conv_b.reshape(1, 2 * dff)
    return pl.pallas_call(
        _ffn_prompt_kernel,
        grid=(m // tm, nf),
        in_specs=[pl.BlockSpec((tm, d), lambda i, j: (i, 0)),
                  pl.BlockSpec((tm, d), lambda i, j: (i, 0)),
                  pl.BlockSpec((d, tf), lambda i, j: (0, j)),
                  pl.BlockSpec((d, tf), lambda i, j: (0, nf + j)),
                  pl.BlockSpec((conv_w.shape[0], tf), lambda i, j: (0, j)),
                  pl.BlockSpec((conv_w.shape[0], tf), lambda i, j: (0, nf + j)),
                  pl.BlockSpec((1, tf), lambda i, j: (0, j)),
                  pl.BlockSpec((1, tf), lambda i, j: (0, nf + j)),
                  pl.BlockSpec((tf, d), lambda i, j: (j, 0)),
                  pl.BlockSpec((1, d), lambda i, j: (0, 0))],
        out_specs=[pl.BlockSpec((tm, d), lambda i, j: (i, 0)),
                   pl.BlockSpec((HALO, tf), lambda i, j: (i, j)),
                   pl.BlockSpec((HALO, tf), lambda i, j: (i, j))],
        out_shape=[jax.ShapeDtypeStruct((m, d), F32),
                   jax.ShapeDtypeStruct((m // tm * HALO, dff), F32),
                   jax.ShapeDtypeStruct((m // tm * HALO, dff), F32)],
        scratch_shapes=[pltpu.VMEM((tm, d), F32), pltpu.VMEM((tm + HALO, tf), F32),
                        pltpu.VMEM((tm + HALO, tf), F32), pltpu.VMEM((nf, HALO, tf), F32),
                        pltpu.VMEM((nf, HALO, tf), F32)],
        compiler_params=_cparams(("arbitrary", "arbitrary")),
        name="ffn_prompt",
    )(a2, h, w_up, w_up, conv_w, conv_w, cb, cb, w_down, n_post)


def _ffn_sample_kernel(a_ref, h_ref, h0g_ref, h1g_ref, h0v_ref, h1v_ref, wg_ref, wv_ref, cwg_ref, cwv_ref,
                       cbg_ref, cbv_ref, wd_ref, nw_ref, y_ref, ug_ref, uv_ref, acc_ref):
    j = pl.program_id(0)

    @pl.when(j == 0)
    def _():
        acc_ref[...] = jnp.zeros(acc_ref.shape, F32)

    a = a_ref[...]

    def conv_half(w_ref, cw_ref, cb_ref, h0_ref, h1_ref, u_ref):
        u = _dot(a, w_ref[...])
        u_ref[...] = u
        return (h0_ref[...] * cw_ref[0:1, :] + h1_ref[...] * cw_ref[1:2, :] + u * cw_ref[2:3, :]
                + cb_ref[...])

    cg = conv_half(wg_ref, cwg_ref, cbg_ref, h0g_ref, h1g_ref, ug_ref)
    cv = conv_half(wv_ref, cwv_ref, cbv_ref, h0v_ref, h1v_ref, uv_ref)
    act = (_gelu_tanh(cg) * cv).astype(BF16)
    acc_ref[...] += _dot(act, wd_ref[...])

    @pl.when(j == pl.num_programs(0) - 1)
    def _():
        y_ref[...] = h_ref[...] + _rms(acc_ref[...], nw_ref[...])


def _ffn_sample_call(a2, h, hist, w_up, conv_w, conv_b, w_down, n_post, tf):
    b, d = h.shape
    dff = w_down.shape[0]
    nf = dff // tf
    cb = conv_b.reshape(1, 2 * dff)
    full = lambda w: pl.BlockSpec((b, w), lambda j: (0, 0))
    col = lambda base: pl.BlockSpec((b, tf), lambda j: (0, base + j))
    prev = lambda r, base: pl.BlockSpec((None, b, tf), lambda j: (r, 0, base + j))
    return pl.pallas_call(
        _ffn_sample_kernel,
        grid=(nf,),
        in_specs=[full(d), full(d), prev(0, 0), prev(1, 0), prev(0, nf), prev(1, nf),
                  pl.BlockSpec((d, tf), lambda j: (0, j)),
                  pl.BlockSpec((d, tf), lambda j: (0, nf + j)),
                  pl.BlockSpec((conv_w.shape[0], tf), lambda j: (0, j)),
                  pl.BlockSpec((conv_w.shape[0], tf), lambda j: (0, nf + j)),
                  pl.BlockSpec((1, tf), lambda j: (0, j)),
                  pl.BlockSpec((1, tf), lambda j: (0, nf + j)),
                  pl.BlockSpec((tf, d), lambda j: (j, 0)),
                  pl.BlockSpec((1, d), lambda j: (0, 0))],
        out_specs=[full(d), col(0), col(0)],
        out_shape=[jax.ShapeDtypeStruct((b, d), F32), jax.ShapeDtypeStruct((b, dff), F32),
                   jax.ShapeDtypeStruct((b, dff), F32)],
        scratch_shapes=[pltpu.VMEM((b, d), F32)],
        compiler_params=_cparams(("arbitrary",)),
        name="ffn_sample",
    )(a2, h, hist, hist, hist, hist, w_up, w_up, conv_w, conv_w, cb, cb, w_down, n_post)


def _rope_tables(pos, half):
    inv = ROPE_BASE ** (-2.0 * np.arange(half, dtype=np.float64) / (2 * half))
    ang = np.asarray(pos, np.float64)[:, None] * inv[None, :]
    zeros = np.zeros((ang.shape[0], LANES - 2 * half))
    cos, sin = np.cos(ang), np.sin(ang)
    return (jnp.asarray(np.concatenate([cos, cos, zeros], axis=1), F32),
            jnp.asarray(np.concatenate([sin, sin, zeros], axis=1), F32))


def _tile(n, pref):
    return pref if n % pref == 0 else n


def kernel(x_prompt, x_sample, cache_kv_latent, cache_k_rope, state_gla, state_ffn_conv, page_table,
           norm_attn_pre, w_in, q_norm, w_uq, kv_norm, w_uk, w_uv, w_alpha, b_alpha, gla_norm, w_o,
           norm_attn_post, norm_ffn_pre, w_up, conv_w, conv_b, w_down, norm_ffn_post):
    depth = w_in.shape[0]
    _, seq, d = x_prompt.shape
    bs, t_new, _ = x_sample.shape
    assert x_prompt.shape[0] == 1 and t_new == 1
    q_rank = w_uq.shape[1]
    kv_rank, heads, nope = w_uk.shape[1:]
    v_head = w_uv.shape[3]
    rope = cache_k_rope.shape[3]
    half = rope // 2
    assert w_uq.shape[2] == heads * (nope + rope) and nope == LANES and v_head == LANES and rope <= LANES // 2
    _, _, gla_heads, dk, dv = state_gla.shape
    gla_qk, mix_gla = gla_heads * dk, gla_heads * dv
    gate_rank = w_alpha.shape[1]
    assert rope + gate_rank <= LANES
    dff = w_down.shape[1]
    assert conv_w.shape[1] == 3
    past = page_table.shape[1] * cache_kv_latent.shape[2]
    mla_scale = float((nope + rope) ** -0.5)
    dims = (q_rank, kv_rank, heads, nope, gla_qk, mix_gla, mla_scale, float(dk ** -0.5))

    ct_p, st_p = _rope_tables(np.arange(seq), half)
    ct_s, st_s = _rope_tables(np.full((bs,), past), half)
    cache_rope_t = jnp.swapaxes(cache_k_rope, 2, 3)
    n_pages = page_table.shape[1]
    pages_per_step = next(p for p in (16, 8, 4, 2) if n_pages % (p * DECODE_SLOTS) == 0)

    yp, ys = x_prompt[0], x_sample[:, 0]
    outs = {k: [] for k in ("lat_p", "rope_p", "gla_p", "conv_p", "lat_s", "rope_s", "gla_s", "conv_s")}
    for l in range(depth):
        sizes = (q_rank, kv_rank, rope, gla_qk, gla_qk, mix_gla, gate_rank, mix_gla)
        idx = np.cumsum(sizes)[:-1].tolist()
        starts = [0] + idx
        w_in_t = jnp.swapaxes(w_in[l], 0, 1)
        narrow = _regroup_call(w_in_t, [starts[2], starts[6]], LANES)
        w_kr, w_ga = narrow[:, 0:rope], narrow[:, LANES:LANES + gate_rank]
        zc = lambda n: jnp.zeros((d, n), BF16)
        wtail = jnp.concatenate([w_kr, w_ga, zc(LANES - rope - gate_rank),
                                 -w_kr[:, half:], w_kr[:, :half], zc(LANES - rope)], axis=1)
        group_starts = [starts[g] + o for g in (0, 1, 3, 4, 5, 7) for o in range(0, sizes[g], REGROUP_WIDTH)]
        win = _regroup_call(w_in_t, group_starts, REGROUP_WIDTH)
        wq3 = w_uq[l].reshape(q_rank, heads, nope + rope)
        wq_n = wq3[:, :, :nope].reshape(q_rank, heads * nope)
        x1, x2 = wq3[:, :, nope:nope + half], wq3[:, :, nope + half:]
        zq = jnp.zeros((q_rank, heads, LANES - rope), F32)
        wq_r = jnp.concatenate([x1, x2, zq], axis=2).reshape(q_rank, heads * LANES)
        wq_rs = jnp.concatenate([-x2, x1, zq], axis=2).reshape(q_rank, heads * LANES)
        wq = jnp.concatenate([wq_n, wq_r, wq_rs], axis=1).astype(BF16)
        w_uk2 = w_uk[l].reshape(kv_rank, heads * nope).astype(BF16)
        w_uv2 = w_uv[l].reshape(kv_rank, heads * v_head).astype(BF16)
        w_ukT = jnp.transpose(w_uk[l], (1, 2, 0)).astype(BF16)
        w_uv_h = jnp.transpose(w_uv[l], (1, 0, 2)).astype(BF16)
        walpha = jnp.zeros((LANES, gla_qk), F32).at[rope:rope + gate_rank].set(w_alpha[l]).astype(BF16)
        row = lambda v: v.reshape(1, -1)
        common = (row(norm_attn_pre[l]), win, wtail, row(q_norm[l]), row(kv_norm[l]), wq)
        tail = (walpha, row(b_alpha[l]))
        wo_b = w_o[l].astype(BF16)
        w_up_b = w_up[l].astype(BF16)
        w_down_b = w_down[l].astype(BF16)
        gnw = row(gla_norm[l])

        tm = _tile(seq, 256)
        (q, k, v, ckv_p, kr_p, gq, gk, gv, g, gg) = _proj_call(
            yp, ct_p, st_p, common + (w_uk2, w_uv2) + tail, dims, rope, False, tm)
        o_mla = _flash_call(q, k, v, heads, nope + LANES, v_head, _tile(seq, 1024))
        o_gla, gla_new_p = _gla_chunk_call(gq, gk, gv, g, gg, gnw, gla_heads, dk, dv, _tile(seq, 128))
        h_p, a2_p = _post_attn_call(yp, o_mla, o_gla, wo_b, row(norm_attn_post[l]), row(norm_ffn_pre[l]),
                                    _tile(seq, 512))
        yp, tail_g, tail_v = _ffn_prompt_call(a2_p, h_p, w_up_b, conv_w[l], conv_b[l], w_down_b,
                                              row(norm_ffn_post[l]), _tile(seq, 512), _tile(dff, 512))
        conv_new_p = jnp.concatenate([tail_g[-2:], tail_v[-2:]], axis=1)[None]

        (q_abs, q_rope, ckv_s, kr_s, sq, sk, sv, sg, sgg) = _proj_call(
            ys, ct_s, st_s, common + (w_ukT, w_uv2) + tail, dims, rope, True, bs)
        lat_out = _decode_call(page_table, q_abs.reshape(bs, heads, kv_rank), q_rope.reshape(bs, heads, LANES),
                               ckv_s.reshape(bs, 1, kv_rank), kr_s.reshape(bs, 1, rope),
                               cache_kv_latent, cache_rope_t, l, pages_per_step)
        o_mla_s = _uv_call(lat_out.reshape(bs, heads * kv_rank), w_uv_h)
        sh = lambda z, w: z.reshape(bs, gla_heads, w)
        o_gla_s, gla_new_s = _gla_step_call(sh(sq, dk), sh(sk, dk), sh(sg, dk), sh(sv, dv), sh(sgg, dv), gnw,
                                            state_gla, l)
        h_s, a2_s = _post_attn_call(ys, o_mla_s, o_gla_s.reshape(bs, mix_gla), wo_b, row(norm_attn_post[l]),
                                    row(norm_ffn_pre[l]), bs)
        hist_t = jnp.swapaxes(state_ffn_conv[l], 0, 1)
        ys, u_g, u_v = _ffn_sample_call(a2_s, h_s, hist_t, w_up_b, conv_w[l], conv_b[l], w_down_b,
                                        row(norm_ffn_post[l]), _tile(dff, 512))
        conv_new_s = jnp.swapaxes(jnp.stack([hist_t[1], jnp.concatenate([u_g, u_v], axis=1)], axis=0), 0, 1)

        outs["lat_p"].append(ckv_p[None]); outs["rope_p"].append(kr_p[None])
        outs["gla_p"].append(gla_new_p[None].astype(x_prompt.dtype)); outs["conv_p"].append(conv_new_p)
        outs["lat_s"].append(ckv_s[:, None]); outs["rope_s"].append(kr_s[:, None])
        outs["gla_s"].append(gla_new_s.astype(state_gla.dtype)); outs["conv_s"].append(conv_new_s)

    st = lambda name: jnp.stack(outs[name])
    return (yp[None], ys[:, None], st("lat_p"), st("rope_p"), st("gla_p"), st("conv_p"),
            st("lat_s"), st("rope_s"), st("gla_s"), st("conv_s"))
```

```python
import functools

import numpy as np
import jax
import jax.numpy as jnp
from jax import lax
from jax.experimental import pallas as pl
from jax.experimental.pallas import tpu as pltpu

EPS = 1e-6
ROPE_BASE = 10000.0
GATE_NORM = 16.0
LANES = 128
V7X_VMEM_LIMIT_BYTES = 56 * 1024 * 1024

F32 = jnp.float32
BF16 = jnp.bfloat16
NEG_INF = float("-inf")


def _cparams(sem):
    return pltpu.CompilerParams(dimension_semantics=sem, vmem_limit_bytes=V7X_VMEM_LIMIT_BYTES)


def _rms(x, w):
    return x * lax.rsqrt(jnp.mean(x * x, axis=-1, keepdims=True) + EPS) * w


def _dot(a, b):
    return jnp.dot(a, b, preferred_element_type=F32)


def _dot_nt(a, b):
    return lax.dot_general(a, b, (((1,), (1,)), ((), ())), preferred_element_type=F32)


def _dot_tn(a, b):
    return lax.dot_general(a, b, (((0,), (0,)), ((), ())), preferred_element_type=F32)


def _const_spec(shape):
    nd = len(shape)
    return pl.BlockSpec(shape, lambda *_: (0,) * nd, pipeline_mode=pl.Buffered(1))


REGROUP_WIDTH = 512
SUBLANES = 8


def _regroup_kernel(st_ref, x_ref, o_ref):
    o_ref[...] = jnp.transpose(x_ref[...]).astype(o_ref.dtype)


def _regroup_call(w_t, starts, width):
    cin, r = w_t.shape
    assert all(s % SUBLANES == 0 and s + width <= cin for s in starts)
    n = len(starts)
    grid_spec = pltpu.PrefetchScalarGridSpec(
        num_scalar_prefetch=1,
        grid=(n,),
        in_specs=[pl.BlockSpec((pl.Element(width), pl.Element(r)), lambda g, st: (st[g] * SUBLANES, 0))],
        out_specs=pl.BlockSpec((r, width), lambda g, st: (0, g)),
    )
    return pl.pallas_call(
        _regroup_kernel,
        grid_spec=grid_spec,
        out_shape=jax.ShapeDtypeStruct((r, n * width), BF16),
        compiler_params=_cparams(("parallel",)),
        name="regroup_w_in",
    )(jnp.asarray([s // SUBLANES for s in starts], jnp.int32), w_t)


def _proj_kernel(dims, sample, x_ref, ct_ref, st_ref, wn_ref, win_ref, wtail_ref, qn_ref, kvn_ref, wq_ref,
                 wa_ref, wb_ref, walpha_ref, balpha_ref, *outs):
    (q_rank, kv_rank, heads, nope, gla_qk, mix_gla, mla_scale, gla_qscale) = dims
    a = _rms(x_ref[...], wn_ref[...]).astype(BF16)

    off = [0]

    def mm(width):
        lo = off[0]
        off[0] += width
        return _dot(a, win_ref[:, lo:lo + width])

    cq = mm(q_rank)
    ckv = mm(kv_rank)
    gq = mm(gla_qk)
    gk = mm(gla_qk)
    gv = mm(mix_gla)
    gg = mm(mix_gla)
    lb = _dot(a, wtail_ref[:, 0:LANES])
    lbs = _dot(a, wtail_ref[:, LANES:2 * LANES])

    ct = ct_ref[...]
    st = st_ref[...]

    cqn = _rms(cq, qn_ref[...]).astype(BF16)
    hn = heads * nope
    q_nope = _dot(cqn, wq_ref[:, 0:hn]) * mla_scale
    q_r = _dot(cqn, wq_ref[:, hn:hn + heads * LANES])
    q_rs = _dot(cqn, wq_ref[:, hn + heads * LANES:hn + 2 * heads * LANES])

    c_kv = _rms(ckv, kvn_ref[...])
    c_kv_b = c_kv.astype(BF16)
    k_rope_blk = lb * ct + lbs * st

    glog = _dot(lb.astype(BF16), walpha_ref[...]) + balpha_ref[...]
    g = (jnp.minimum(glog, 0.0) - jnp.log1p(jnp.exp(-jnp.abs(glog)))) * (1.0 / GATE_NORM)

    if sample:
        qabs_ref, qrope_ref, ckv_ref, kr_ref, gq_ref, gk_ref, gv_ref, g_ref, gg_ref = outs
        for h in range(heads):
            qn_h = q_nope[:, h * nope:(h + 1) * nope].astype(BF16)
            qabs_ref[:, h * kv_rank:(h + 1) * kv_rank] = _dot(qn_h, wa_ref[h]).astype(BF16)
            blk = slice(h * LANES, (h + 1) * LANES)
            qrope_ref[:, blk] = ((q_r[:, blk] * ct + q_rs[:, blk] * st) * mla_scale).astype(BF16)
    else:
        q_ref, k_ref, v_ref, ckv_ref, kr_ref, gq_ref, gk_ref, gv_ref, g_ref, gg_ref = outs
        k_nope = _dot(c_kv_b, wa_ref[...])
        v_ref[...] = _dot(c_kv_b, wb_ref[...]).astype(BF16)
        k_rope_b = k_rope_blk.astype(BF16)
        for h in range(heads):
            blk = slice(h * LANES, (h + 1) * LANES)
            base = h * (nope + LANES)
            q_ref[:, base:base + nope] = q_nope[:, h * nope:(h + 1) * nope].astype(BF16)
            q_ref[:, base + nope:base + nope + LANES] = (
                (q_r[:, blk] * ct + q_rs[:, blk] * st) * mla_scale).astype(BF16)
            k_ref[:, base:base + nope] = k_nope[:, h * nope:(h + 1) * nope].astype(BF16)
            k_ref[:, base + nope:base + nope + LANES] = k_rope_b

    ckv_ref[...] = c_kv
    kr_ref[...] = k_rope_blk[:, 0:kr_ref.shape[1]]
    gq_ref[...] = gq * gla_qscale
    gk_ref[...] = gk
    gv_ref[...] = gv
    g_ref[...] = g
    gg_ref[...] = gg


def _proj_call(x, ct, st, wts, dims, rope, sample, tm):
    (q_rank, kv_rank, heads, nope, gla_qk, mix_gla, _, _) = dims
    m, d = x.shape
    row = lambda w: pl.BlockSpec((tm, w), lambda i: (i, 0))
    in_specs = [row(d), row(LANES), row(LANES)] + [_const_spec(w.shape) for w in wts]
    gla_shapes = [(m, gla_qk), (m, gla_qk), (m, mix_gla), (m, gla_qk), (m, mix_gla)]
    if sample:
        out_shapes = [((m, heads * kv_rank), BF16), ((m, heads * LANES), BF16)]
    else:
        out_shapes = [((m, heads * (nope + LANES)), BF16), ((m, heads * (nope + LANES)), BF16),
                      ((m, heads * nope), BF16)]
    out_shapes += [((m, kv_rank), F32), ((m, rope), F32)] + [(s, F32) for s in gla_shapes]
    return pl.pallas_call(
        functools.partial(_proj_kernel, dims, sample),
        grid=(m // tm,),
        in_specs=in_specs,
        out_specs=[row(s[1]) for s, _ in out_shapes],
        out_shape=[jax.ShapeDtypeStruct(s, dt) for s, dt in out_shapes],
        compiler_params=_cparams(("parallel",)),
        name="proj_sample" if sample else "proj_prompt",
    )(x, ct, st, *wts)


FLASH_SUB = 512


def _flash_kernel(qi_ref, kj_ref, q_ref, k_ref, v_ref, o_ref, m_ref, l_ref, acc_ref):
    t = pl.program_id(1)
    i = qi_ref[t]
    j = kj_ref[t]
    tq, tk = q_ref.shape[0], k_ref.shape[0]

    @pl.when(j == 0)
    def _():
        m_ref[...] = jnp.full(m_ref.shape, NEG_INF, F32)
        l_ref[...] = jnp.zeros(l_ref.shape, F32)
        acc_ref[...] = jnp.zeros(acc_ref.shape, F32)

    def update(masked):
        ts = FLASH_SUB if tk % FLASH_SUB == 0 else tk
        q = q_ref[...]
        m, l, acc = m_ref[...], l_ref[...], acc_ref[...]
        for c0 in range(0, tk, ts):
            r0 = c0 if masked else 0
            s = _dot_nt(q[r0:], k_ref[c0:c0 + ts, :])
            if masked:
                rowi = lax.broadcasted_iota(jnp.int32, (tq - r0, ts), 0) + r0
                coli = lax.broadcasted_iota(jnp.int32, (tq - r0, ts), 1) + c0
                s = jnp.where(coli <= rowi, s, NEG_INF)
            m_new = jnp.maximum(m[r0:], jnp.max(s, axis=-1, keepdims=True))
            alpha = jnp.exp(m[r0:] - m_new)
            p = jnp.exp(s - jnp.tile(m_new, (1, ts // LANES)))
            l_new = alpha * l[r0:] + jnp.sum(p, axis=-1, keepdims=True)
            acc_new = alpha * acc[r0:] + _dot(p.astype(BF16), v_ref[c0:c0 + ts, :])
            keep = lambda old, new: jnp.concatenate([old[:r0], new], axis=0) if r0 else new
            m, l, acc = keep(m, m_new), keep(l, l_new), keep(acc, acc_new)
        m_ref[...], l_ref[...], acc_ref[...] = m, l, acc

    @pl.when(j < i)
    def _():
        update(False)

    @pl.when(j == i)
    def _():
        update(True)
        o_ref[...] = (acc_ref[...] / l_ref[...]).astype(o_ref.dtype)


def _flash_call(q, k, v, heads, qk_w, v_w, t):
    assert v_w == LANES
    s = q.shape[0]
    n = s // t
    pairs = [(i, j) for i in range(n) for j in range(i + 1)]
    qi = jnp.asarray([p[0] for p in pairs], jnp.int32)
    kj = jnp.asarray([p[1] for p in pairs], jnp.int32)
    grid_spec = pltpu.PrefetchScalarGridSpec(
        num_scalar_prefetch=2,
        grid=(heads, len(pairs)),
        in_specs=[pl.BlockSpec((t, qk_w), lambda h, p, qi, kj: (qi[p], h)),
                  pl.BlockSpec((t, qk_w), lambda h, p, qi, kj: (kj[p], h)),
                  pl.BlockSpec((t, v_w), lambda h, p, qi, kj: (kj[p], h))],
        out_specs=pl.BlockSpec((t, v_w), lambda h, p, qi, kj: (qi[p], h)),
        scratch_shapes=[pltpu.VMEM((t, LANES), F32), pltpu.VMEM((t, LANES), F32),
                        pltpu.VMEM((t, v_w), F32)],
    )
    return pl.pallas_call(
        _flash_kernel,
        grid_spec=grid_spec,
        out_shape=jax.ShapeDtypeStruct((s, heads * v_w), BF16),
        compiler_params=_cparams(("parallel", "arbitrary")),
        name="mla_prompt_flash",
    )(qi, kj, q, k, v)


def _gla_chunk_kernel(heads, dk, dv, q_ref, k_ref, v_ref, g_ref, gg_ref, nw_ref, o_ref, sfin_ref,
                      state_ref):
    c = pl.program_id(0)
    ch = q_ref.shape[0]

    @pl.when(c == 0)
    def _():
        state_ref[...] = jnp.zeros(state_ref.shape, F32)

    rowi = lax.broadcasted_iota(jnp.int32, (ch, ch), 0)
    coli = lax.broadcasted_iota(jnp.int32, (ch, ch), 1)
    causal = coli <= rowi
    tri = causal.astype(BF16)
    nw = nw_ref[...]
    for h in range(heads):
        ks = slice(h * dk, (h + 1) * dk)
        vs = slice(h * dv, (h + 1) * dv)
        g = g_ref[:, ks]
        g_hi = g.astype(BF16)
        g_mid = (g - g_hi.astype(F32)).astype(BF16)
        g_lo = (g - g_hi.astype(F32) - g_mid.astype(F32)).astype(BF16)
        b = _dot(tri, g_hi) + _dot(tri, g_mid) + _dot(tri, g_lo)
        b_last = b[ch - 1:ch, :]
        b_mid = b[ch // 2:ch // 2 + 1, :]
        q = q_ref[:, ks]
        qb = (q * jnp.exp(b)).astype(BF16)
        qm = (q * jnp.exp(b - b_mid)).astype(BF16)
        km = (k_ref[:, ks] * jnp.exp(b_mid - b)).astype(BF16)
        kl = (k_ref[:, ks] * jnp.exp(b_last - b)).astype(BF16)
        vb = v_ref[:, vs].astype(BF16)
        state = state_ref[h]
        attn = jnp.where(causal, _dot_nt(qm, km), 0.0)
        o = _dot(qb, state.astype(BF16)) + _dot(attn.astype(BF16), vb)
        decay_col = jnp.transpose(jnp.broadcast_to(jnp.exp(b_last), (dk, dk)))[:, 0:1]
        state_ref[h] = decay_col * state + _dot_tn(kl, vb)
        gate = gg_ref[:, vs]
        o_ref[:, vs] = (_rms(o, nw) * (gate * jax.nn.sigmoid(gate))).astype(o_ref.dtype)

    @pl.when(c == pl.num_programs(0) - 1)
    def _():
        sfin_ref[...] = state_ref[...]


def _gla_chunk_call(gq, gk, gv, g, gg, nw, heads, dk, dv, ch):
    s = gq.shape[0]
    row = lambda w: pl.BlockSpec((ch, w), lambda c: (c, 0))
    return pl.pallas_call(
        functools.partial(_gla_chunk_kernel, heads, dk, dv),
        grid=(s // ch,),
        in_specs=[row(heads * dk), row(heads * dk), row(heads * dv), row(heads * dk), row(heads * dv),
                  pl.BlockSpec((1, dv), lambda c: (0, 0))],
        out_specs=[row(heads * dv), pl.BlockSpec((heads, dk, dv), lambda c: (0, 0, 0))],
        out_shape=[jax.ShapeDtypeStruct((s, heads * dv), BF16),
                   jax.ShapeDtypeStruct((heads, dk, dv), F32)],
        scratch_shapes=[pltpu.VMEM((heads, dk, dv), F32)],
        compiler_params=_cparams(("arbitrary",)),
        name="gla_prompt_chunked",
    )(gq, gk, gv, g, gg, nw)


DECODE_SLOTS = 8
DECODE_AHEAD = 6


def _decode_kernel(pages, n_chunks, layer, pt_ref, qa_ref, qr_ref, cn_ref, kn_ref, lat_hbm, rk_hbm,
                   o_ref, lat_buf, rk_buf, sem):
    b = pl.program_id(0)
    total = pl.num_programs(0) * n_chunks
    heads, kv_rank = o_ref.shape
    pg = lat_hbm.shape[2]
    rope_w = rk_hbm.shape[2]
    reps = kv_rank // LANES

    def chunk_copies(bi, ci, slot):
        cps = []
        for p in range(pages):
            phys = pt_ref[bi, ci * pages + p]
            cps.append(pltpu.make_async_copy(lat_hbm.at[layer, phys], lat_buf.at[slot, pl.ds(p * pg, pg)],
                                             sem.at[0, slot]))
            cps.append(pltpu.make_async_copy(rk_hbm.at[layer, phys], rk_buf.at[slot, p], sem.at[1, slot]))
        return cps

    def start_chunk(bi, ci):
        for n, cp in enumerate(chunk_copies(bi, ci, ci % DECODE_SLOTS)):
            cp.start(priority=(n // 2) % 2)

    @pl.when(b == 0)
    def _():
        for c in range(DECODE_AHEAD):
            start_chunk(0, c)

    qa = qa_ref[...].astype(F32)
    qt = jnp.transpose(jnp.concatenate([qa, jnp.zeros((LANES - heads, kv_rank), F32)], axis=0))
    qr = qr_ref[:, 0:rope_w].astype(F32)
    half = pages * pg // 2

    def latent_scores(c):
        slot = c % DECODE_SLOTS
        for cp in chunk_copies(b, c, slot):
            cp.wait()
        return (_dot(lat_buf[slot, 0:half], qt), _dot(lat_buf[slot, half:], qt))

    m = jnp.full((heads, LANES), NEG_INF, F32)
    l = jnp.zeros((heads, LANES), F32)
    acc = jnp.zeros((heads, kv_rank), F32)
    st = latent_scores(0)
    for c in range(n_chunks):
        slot = c % DECODE_SLOTS
        ahead = c + DECODE_AHEAD
        nb, nc = (b, ahead) if ahead < n_chunks else (b + 1, ahead - n_chunks)

        @pl.when(b * n_chunks + ahead < total)
        def _():
            start_chunk(nb, nc)

        st_next = latent_scores(c + 1) if c + 1 < n_chunks else None
        per_half = pages // 2
        s = jnp.concatenate(
            [jnp.transpose(st[p // per_half][(p % per_half) * pg:(p % per_half + 1) * pg, :])[0:heads, :]
             + _dot(qr, rk_buf[slot, p]) for p in range(pages)], axis=1)
        m_new = jnp.maximum(m, jnp.max(s, axis=-1, keepdims=True))
        alpha = jnp.exp(m - m_new)
        p_un = jnp.exp(s - jnp.tile(m_new, (1, pages * pg // LANES)))
        l = alpha * l + jnp.sum(p_un, axis=-1, keepdims=True)
        acc = jnp.tile(alpha, (1, reps)) * acc + _dot(p_un, lat_buf[slot])
        m = m_new
        st = st_next

    cn = cn_ref[...]
    kn = kn_ref[...]
    s_self = (jnp.sum(qa * cn, axis=-1, keepdims=True)
              + jnp.sum(qr * kn, axis=-1, keepdims=True))
    m_new = jnp.maximum(m, s_self)
    alpha = jnp.exp(m - m_new)
    p_self = jnp.exp(s_self - m_new)
    l_fin = alpha * l + p_self
    acc_fin = jnp.tile(alpha, (1, reps)) * acc + jnp.tile(p_self, (1, reps)) * cn
    o_ref[...] = (acc_fin / jnp.tile(l_fin, (1, reps))).astype(o_ref.dtype)


def _decode_call(page_table, q_abs, q_rope, ckv_new, kr_new, cache_lat, cache_rope_t, layer, pages):
    b, heads, kv_rank = q_abs.shape
    n_pages = page_table.shape[1]
    page = cache_lat.shape[2]
    rope_w = cache_rope_t.shape[2]
    n_chunks = n_pages // pages
    assert n_chunks % DECODE_SLOTS == 0 and DECODE_AHEAD < DECODE_SLOTS - 1 and pages % 2 == 0

    per_b = lambda *shape: pl.BlockSpec((None,) + shape, lambda bi, pt: (bi,) + (0,) * len(shape))
    grid_spec = pltpu.PrefetchScalarGridSpec(
        num_scalar_prefetch=1,
        grid=(b,),
        in_specs=[per_b(heads, kv_rank), per_b(heads, LANES), per_b(1, kv_rank),
                  per_b(1, rope_w), pl.BlockSpec(memory_space=pl.ANY), pl.BlockSpec(memory_space=pl.ANY)],
        out_specs=per_b(heads, kv_rank),
        scratch_shapes=[pltpu.VMEM((DECODE_SLOTS, pages * page, kv_rank), F32),
                        pltpu.VMEM((DECODE_SLOTS, pages, rope_w, page), F32),
                        pltpu.SemaphoreType.DMA((2, DECODE_SLOTS))],
    )
    return pl.pallas_call(
        functools.partial(_decode_kernel, pages, n_chunks, layer),
        grid_spec=grid_spec,
        out_shape=jax.ShapeDtypeStruct((b, heads, kv_rank), BF16),
        compiler_params=_cparams(("arbitrary",)),
        name="mla_paged_decode",
    )(page_table, q_abs, q_rope, ckv_new, kr_new, cache_lat, cache_rope_t)


def _uv_kernel(x_ref, w_ref, o_ref):
    o_ref[...] = _dot(x_ref[...], w_ref[...]).astype(o_ref.dtype)


def _uv_call(lat_out, w_uv_h):
    heads, c, dv = w_uv_h.shape
    b = lat_out.shape[0]
    return pl.pallas_call(
        _uv_kernel,
        grid=(heads,),
        in_specs=[pl.BlockSpec((b, c), lambda h: (0, h)),
                  pl.BlockSpec((None, c, dv), lambda h: (h, 0, 0))],
        out_specs=pl.BlockSpec((b, dv), lambda h: (0, h)),
        out_shape=jax.ShapeDtypeStruct((b, heads * dv), BF16),
        compiler_params=_cparams(("parallel",)),
        name="mla_decode_value_up",
    )(lat_out, w_uv_h)


def _gla_step_kernel(heads, q_ref, k_ref, g_ref, v_ref, gg_ref, nw_ref, s_ref, o_ref, snew_ref):
    bb, _, dk = q_ref.shape
    rows = [r[i] for i in range(bb) for r in (q_ref, k_ref)] + [jnp.exp(g_ref[i]) for i in range(bb)]
    pad = jnp.zeros((dk - 3 * heads * bb, dk), F32)
    cols = jnp.transpose(jnp.concatenate(rows + [pad], axis=0))
    nw = nw_ref[...]
    for i in range(bb):
        for h in range(heads):
            cq, ck, ce = (2 * i) * heads + h, (2 * i + 1) * heads + h, (2 * bb + i) * heads + h
            s_new = cols[:, ce:ce + 1] * s_ref[i, h] + cols[:, ck:ck + 1] * v_ref[i, h:h + 1, :]
            snew_ref[i, h] = s_new
            o = jnp.sum(cols[:, cq:cq + 1] * s_new, axis=0, keepdims=True)
            gate = gg_ref[i, h:h + 1, :]
            o_ref[i, h:h + 1, :] = (_rms(o, nw) * (gate * jax.nn.sigmoid(gate))).astype(o_ref.dtype)


def _gla_step_call(gq, gk, g, gv, gg, nw, state, layer):
    b, heads, dk = gq.shape
    dv = gv.shape[2]
    bb = next(n for n in (8, 4, 2, 1) if b % n == 0 and 3 * heads * n <= dk)
    per_b = lambda w: pl.BlockSpec((bb, heads, w), lambda bi: (bi, 0, 0))
    return pl.pallas_call(
        functools.partial(_gla_step_kernel, heads),
        grid=(b // bb,),
        in_specs=[per_b(dk), per_b(dk), per_b(dk), per_b(dv), per_b(dv),
                  pl.BlockSpec((1, dv), lambda bi: (0, 0)),
                  pl.BlockSpec((None, bb, heads, dk, dv), lambda bi: (layer, bi, 0, 0, 0))],
        out_specs=[per_b(dv), pl.BlockSpec((bb, heads, dk, dv), lambda bi: (bi, 0, 0, 0))],
        out_shape=[jax.ShapeDtypeStruct((b, heads, dv), BF16),
                   jax.ShapeDtypeStruct((b, heads, dk, dv), F32)],
        compiler_params=_cparams(("parallel",)),
        name="gla_sample_step",
    )(gq, gk, g, gv, gg, nw, state)


def _post_attn_kernel(x_ref, om_ref, og_ref, wo_ref, np_ref, nf_ref, h_ref, a_ref):
    nm = om_ref.shape[1]
    mix = _dot(om_ref[...], wo_ref[0:nm, :]) + _dot(og_ref[...], wo_ref[nm:, :])
    h = x_ref[...] + _rms(mix, np_ref[...])
    h_ref[...] = h
    a_ref[...] = _rms(h, nf_ref[...]).astype(a_ref.dtype)


def _post_attn_call(x, o_mla, o_gla, wo, n_post, n_ffn, tm):
    m, d = x.shape
    row = lambda w: pl.BlockSpec((tm, w), lambda i: (i, 0))
    return pl.pallas_call(
        _post_attn_kernel,
        grid=(m // tm,),
        in_specs=[row(d), row(o_mla.shape[1]), row(o_gla.shape[1]), _const_spec(wo.shape),
                  _const_spec(n_post.shape), _const_spec(n_ffn.shape)],
        out_specs=[row(d), row(d)],
        out_shape=[jax.ShapeDtypeStruct((m, d), F32), jax.ShapeDtypeStruct((m, d), BF16)],
        compiler_params=_cparams(("parallel",)),
        name="attn_out_proj",
    )(x, o_mla, o_gla, wo, n_post, n_ffn)


HALO = 8

def _gelu_tanh(x):
    return 0.5 * x * (1.0 + jnp.tanh(np.sqrt(2.0 / np.pi) * (x + 0.044715 * (x * x * x))))


def _ffn_prompt_kernel(a_ref, h_ref, wg_ref, wv_ref, cwg_ref, cwv_ref, cbg_ref, cbv_ref, wd_ref,
                       nw_ref, y_ref, tg_ref, tv_ref, acc_ref, ug_ref, uv_ref, pg_ref, pv_ref):
    i = pl.program_id(0)
    j = pl.program_id(1)
    tm = a_ref.shape[0]

    @pl.when(j == 0)
    def _():
        acc_ref[...] = jnp.zeros(acc_ref.shape, F32)

    @pl.when(i == 0)
    def _():
        pg_ref[j] = jnp.zeros(pg_ref.shape[1:], F32)
        pv_ref[j] = jnp.zeros(pv_ref.shape[1:], F32)

    a = a_ref[...]

    def conv_half(w_ref, cw_ref, cb_ref, u_ref, tail_ref, prev_ref):
        u_ref[0:HALO, :] = prev_ref[j]
        u = _dot(a, w_ref[...])
        u_ref[HALO:, :] = u
        tail = u[tm - HALO:, :]
        tail_ref[...] = tail
        prev_ref[j] = tail
        return (u_ref[pl.ds(HALO - 2, tm), :] * cw_ref[0:1, :] + u_ref[pl.ds(HALO - 1, tm), :] * cw_ref[1:2, :]
                + u * cw_ref[2:3, :] + cb_ref[...])

    cg = conv_half(wg_ref, cwg_ref, cbg_ref, ug_ref, tg_ref, pg_ref)
    cv = conv_half(wv_ref, cwv_ref, cbv_ref, uv_ref, tv_ref, pv_ref)
    act = (_gelu_tanh(cg) * cv).astype(BF16)
    acc_ref[...] += _dot(act, wd_ref[...])

    @pl.when(j == pl.num_programs(1) - 1)
    def _():
        y_ref[...] = h_ref[...] + _rms(acc_ref[...], nw_ref[...])


def _ffn_prompt_call(a2, h, w_up, conv_w, conv_b, w_down, n_post, tm, tf):
    m, d = h.shape
    dff = w_down.shape[0]
    nf = dff // tf
    cb = conv_b.reshape(1, 2 * dff)
    return pl.pallas_call(
        _ffn_prompt_kernel,
        grid=(m // tm, nf),
        in_specs=[pl.BlockSpec((tm, d), lambda i, j: (i, 0)),
                  pl.BlockSpec((tm, d), lambda i, j: (i, 0)),
                  pl.BlockSpec((d, tf), lambda i, j: (0, j)),
                  pl.BlockSpec((d, tf), lambda i, j: (0, nf + j)),
                  pl.BlockSpec((conv_w.shape[0], tf), lambda i, j: (0, j)),
                  pl.BlockSpec((conv_w.shape[0], tf), lambda i, j: (0, nf + j)),
                  pl.BlockSpec((1, tf), lambda i, j: (0, j)),
                  pl.BlockSpec((1, tf), lambda i, j: (0, nf + j)),
                  pl.BlockSpec((tf, d), lambda i, j: (j, 0)),
                  pl.BlockSpec((1, d), lambda i, j: (0, 0))],
        out_specs=[pl.BlockSpec((tm, d), lambda i, j: (i, 0)),
                   pl.BlockSpec((HALO, tf), lambda i, j: (i, j)),
                   pl.BlockSpec((HALO, tf), lambda i, j: (i, j))],
        out_shape=[jax.ShapeDtypeStruct((m, d), F32),
                   jax.ShapeDtypeStruct((m // tm * HALO, dff), F32),
                   jax.ShapeDtypeStruct((m // tm * HALO, dff), F32)],
        scratch_shapes=[pltpu.VMEM((tm, d), F32), pltpu.VMEM((tm + HALO, tf), F32),
                        pltpu.VMEM((tm + HALO, tf), F32), pltpu.VMEM((nf, HALO, tf), F32),
                        pltpu.VMEM((nf, HALO, tf), F32)],
        compiler_params=_cparams(("arbitrary", "arbitrary")),
        name="ffn_prompt",
    )(a2, h, w_up, w_up, conv_w, conv_w, cb, cb, w_down, n_post)


def _ffn_sample_kernel(a_ref, h_ref, h0g_ref, h1g_ref, h0v_ref, h1v_ref, wg_ref, wv_ref, cwg_ref, cwv_ref,
                       cbg_ref, cbv_ref, wd_ref, nw_ref, y_ref, ug_ref, uv_ref, acc_ref):
    j = pl.program_id(0)

    @pl.when(j == 0)
    def _():
        acc_ref[...] = jnp.zeros(acc_ref.shape, F32)

    a = a_ref[...]

    def conv_half(w_ref, cw_ref, cb_ref, h0_ref, h1_ref, u_ref):
        u = _dot(a, w_ref[...])
        u_ref[...] = u
        return (h0_ref[...] * cw_ref[0:1, :] + h1_ref[...] * cw_ref[1:2, :] + u * cw_ref[2:3, :]
                + cb_ref[...])

    cg = conv_half(wg_ref, cwg_ref, cbg_ref, h0g_ref, h1g_ref, ug_ref)
    cv = conv_half(wv_ref, cwv_ref, cbv_ref, h0v_ref, h1v_ref, uv_ref)
    act = (_gelu_tanh(cg) * cv).astype(BF16)
    acc_ref[...] += _dot(act, wd_ref[...])

    @pl.when(j == pl.num_programs(0) - 1)
    def _():
        y_ref[...] = h_ref[...] + _rms(acc_ref[...], nw_ref[...])


def _ffn_sample_call(a2, h, hist, w_up, conv_w, conv_b, w_down, n_post, tf):
    b, d = h.shape
    dff = w_down.shape[0]
    nf = dff // tf
    cb = conv_b.reshape(1, 2 * dff)
    full = lambda w: pl.BlockSpec((b, w), lambda j: (0, 0))
    col = lambda base: pl.BlockSpec((b, tf), lambda j: (0, base + j))
    prev = lambda r, base: pl.BlockSpec((None, b, tf), lambda j: (r, 0, base + j))
    return pl.pallas_call(
        _ffn_sample_kernel,
        grid=(nf,),
        in_specs=[full(d), full(d), prev(0, 0), prev(1, 0), prev(0, nf), prev(1, nf),
                  pl.BlockSpec((d, tf), lambda j: (0, j)),
                  pl.BlockSpec((d, tf), lambda j: (0, nf + j)),
                  pl.BlockSpec((conv_w.shape[0], tf), lambda j: (0, j)),
                  pl.BlockSpec((conv_w.shape[0], tf), lambda j: (0, nf + j)),
                  pl.BlockSpec((1, tf), lambda j: (0, j)),
                  pl.BlockSpec((1, tf), lambda j: (0, nf + j)),
                  pl.BlockSpec((tf, d), lambda j: (j, 0)),
                  pl.BlockSpec((1, d), lambda j: (0, 0))],
        out_specs=[full(d), col(0), col(0)],
        out_shape=[jax.ShapeDtypeStruct((b, d), F32), jax.ShapeDtypeStruct((b, dff), F32),
                   jax.ShapeDtypeStruct((b, dff), F32)],
        scratch_shapes=[pltpu.VMEM((b, d), F32)],
        compiler_params=_cparams(("arbitrary",)),
        name="ffn_sample",
    )(a2, h, hist, hist, hist, hist, w_up, w_up, conv_w, conv_w, cb, cb, w_down, n_post)


def _rope_tables(pos, half):
    inv = ROPE_BASE ** (-2.0 * np.arange(half, dtype=np.float64) / (2 * half))
    ang = np.asarray(pos, np.float64)[:, None] * inv[None, :]
    zeros = np.zeros((ang.shape[0], LANES - 2 * half))
    cos, sin = np.cos(ang), np.sin(ang)
    return (jnp.asarray(np.concatenate([cos, cos, zeros], axis=1), F32),
            jnp.asarray(np.concatenate([sin, sin, zeros], axis=1), F32))


def _tile(n, pref):
    return pref if n % pref == 0 else n


def kernel(x_prompt, x_sample, cache_kv_latent, cache_k_rope, state_gla, state_ffn_conv, page_table,
           norm_attn_pre, w_in, q_norm, w_uq, kv_norm, w_uk, w_uv, w_alpha, b_alpha, gla_norm, w_o,
           norm_attn_post, norm_ffn_pre, w_up, conv_w, conv_b, w_down, norm_ffn_post):
    depth = w_in.shape[0]
    _, seq, d = x_prompt.shape
    bs, t_new, _ = x_sample.shape
    assert x_prompt.shape[0] == 1 and t_new == 1
    q_rank = w_uq.shape[1]
    kv_rank, heads, nope = w_uk.shape[1:]
    v_head = w_uv.shape[3]
    rope = cache_k_rope.shape[3]
    half = rope // 2
    assert w_uq.shape[2] == heads * (nope + rope) and nope == LANES and v_head == LANES and rope <= LANES // 2
    _, _, gla_heads, dk, dv = state_gla.shape
    gla_qk, mix_gla = gla_heads * dk, gla_heads * dv
    gate_rank = w_alpha.shape[1]
    assert rope + gate_rank <= LANES
    dff = w_down.shape[1]
    assert conv_w.shape[1] == 3
    past = page_table.shape[1] * cache_kv_latent.shape[2]
    mla_scale = float((nope + rope) ** -0.5)
    dims = (q_rank, kv_rank, heads, nope, gla_qk, mix_gla, mla_scale, float(dk ** -0.5))

    ct_p, st_p = _rope_tables(np.arange(seq), half)
    ct_s, st_s = _rope_tables(np.full((bs,), past), half)
    cache_rope_t = jnp.swapaxes(cache_k_rope, 2, 3)
    n_pages = page_table.shape[1]
    pages_per_step = next(p for p in (16, 8, 4, 2) if n_pages % (p * DECODE_SLOTS) == 0)

    yp, ys = x_prompt[0], x_sample[:, 0]
    outs = {k: [] for k in ("lat_p", "rope_p", "gla_p", "conv_p", "lat_s", "rope_s", "gla_s", "conv_s")}
    for l in range(depth):
        sizes = (q_rank, kv_rank, rope, gla_qk, gla_qk, mix_gla, gate_rank, mix_gla)
        idx = np.cumsum(sizes)[:-1].tolist()
        starts = [0] + idx
        w_in_t = jnp.swapaxes(w_in[l], 0, 1)
        narrow = _regroup_call(w_in_t, [starts[2], starts[6]], LANES)
        w_kr, w_ga = narrow[:, 0:rope], narrow[:, LANES:LANES + gate_rank]
        zc = lambda n: jnp.zeros((d, n), BF16)
        wtail = jnp.concatenate([w_kr, w_ga, zc(LANES - rope - gate_rank),
                                 -w_kr[:, half:], w_kr[:, :half], zc(LANES - rope)], axis=1)
        group_starts = [starts[g] + o for g in (0, 1, 3, 4, 5, 7) for o in range(0, sizes[g], REGROUP_WIDTH)]
        win = _regroup_call(w_in_t, group_starts, REGROUP_WIDTH)
        wq3 = w_uq[l].reshape(q_rank, heads, nope + rope)
        wq_n = wq3[:, :, :nope].reshape(q_rank, heads * nope)
        x1, x2 = wq3[:, :, nope:nope + half], wq3[:, :, nope + half:]
        zq = jnp.zeros((q_rank, heads, LANES - rope), F32)
        wq_r = jnp.concatenate([x1, x2, zq], axis=2).reshape(q_rank, heads * LANES)
        wq_rs = jnp.concatenate([-x2, x1, zq], axis=2).reshape(q_rank, heads * LANES)
        wq = jnp.concatenate([wq_n, wq_r, wq_rs], axis=1).astype(BF16)
        w_uk2 = w_uk[l].reshape(kv_rank, heads * nope).astype(BF16)
        w_uv2 = w_uv[l].reshape(kv_rank, heads * v_head).astype(BF16)
        w_ukT = jnp.transpose(w_uk[l], (1, 2, 0)).astype(BF16)
        w_uv_h = jnp.transpose(w_uv[l], (1, 0, 2)).astype(BF16)
        walpha = jnp.zeros((LANES, gla_qk), F32).at[rope:rope + gate_rank].set(w_alpha[l]).astype(BF16)
        row = lambda v: v.reshape(1, -1)
        common = (row(norm_attn_pre[l]), win, wtail, row(q_norm[l]), row(kv_norm[l]), wq)
        tail = (walpha, row(b_alpha[l]))
        wo_b = w_o[l].astype(BF16)
        w_up_b = w_up[l].astype(BF16)
        w_down_b = w_down[l].astype(BF16)
        gnw = row(gla_norm[l])

        tm = _tile(seq, 256)
        (q, k, v, ckv_p, kr_p, gq, gk, gv, g, gg) = _proj_call(
            yp, ct_p, st_p, common + (w_uk2, w_uv2) + tail, dims, rope, False, tm)
        o_mla = _flash_call(q, k, v, heads, nope + LANES, v_head, _tile(seq, 2048))
        o_gla, gla_new_p = _gla_chunk_call(gq, gk, gv, g, gg, gnw, gla_heads, dk, dv, _tile(seq, 128))
        h_p, a2_p = _post_attn_call(yp, o_mla, o_gla, wo_b, row(norm_attn_post[l]), row(norm_ffn_pre[l]),
                                    _tile(seq, 512))
        yp, tail_g, tail_v = _ffn_prompt_call(a2_p, h_p, w_up_b, conv_w[l], conv_b[l], w_down_b,
                                              row(norm_ffn_post[l]), _tile(seq, 512), _tile(dff, 512))
        conv_new_p = jnp.concatenate([tail_g[-2:], tail_v[-2:]], axis=1)[None]

        (q_abs, q_rope, ckv_s, kr_s, sq, sk, sv, sg, sgg) = _proj_call(
            ys, ct_s, st_s, common + (w_ukT, w_uv2) + tail, dims, rope, True, bs)
        lat_out = _decode_call(page_table, q_abs.reshape(bs, heads, kv_rank), q_rope.reshape(bs, heads, LANES),
                               ckv_s.reshape(bs, 1, kv_rank), kr_s.reshape(bs, 1, rope),
                               cache_kv_latent, cache_rope_t, l, pages_per_step)
        o_mla_s = _uv_call(lat_out.reshape(bs, heads * kv_rank), w_uv_h)
        sh = lambda z, w: z.reshape(bs, gla_heads, w)
        o_gla_s, gla_new_s = _gla_step_call(sh(sq, dk), sh(sk, dk), sh(sg, dk), sh(sv, dv), sh(sgg, dv), gnw,
                                            state_gla, l)
        h_s, a2_s = _post_attn_call(ys, o_mla_s, o_gla_s.reshape(bs, mix_gla), wo_b, row(norm_attn_post[l]),
                                    row(norm_ffn_pre[l]), bs)
        hist_t = jnp.swapaxes(state_ffn_conv[l], 0, 1)
        ys, u_g, u_v = _ffn_sample_call(a2_s, h_s, hist_t, w_up_b, conv_w[l], conv_b[l], w_down_b,
                                        row(norm_ffn_post[l]), _tile(dff, 512))
        conv_new_s = jnp.swapaxes(jnp.stack([hist_t[1], jnp.concatenate([u_g, u_v], axis=1)], axis=0), 0, 1)

        outs["lat_p"].append(ckv_p[None]); outs["rope_p"].append(kr_p[None])
        outs["gla_p"].append(gla_new_p[None].astype(x_prompt.dtype)); outs["conv_p"].append(conv_new_p)
        outs["lat_s"].append(ckv_s[:, None]); outs["rope_s"].append(kr_s[:, None])
        outs["gla_s"].append(gla_new_s.astype(state_gla.dtype)); outs["conv_s"].append(conv_new_s)

    st = lambda name: jnp.stack(outs[name])
    return (yp[None], ys[:, None], st("lat_p"), st("rope_p"), st("gla_p"), st("conv_p"),
            st("lat_s"), st("rope_s"), st("gla_s"), st("conv_s"))
```
